```python
import math
import jax
import jax.numpy as jnp
from jax import lax
import numpy as np

D_MODEL = 1024
BATCH = 2
SEQ = 16384
DEPTH = 2

GRID_W = 64
CTX_LEN = 256
HEAD_DIM = 64
ROPE_BASE = 10000.0
NORM_EPS = 1e-6
N_ADA = 6

A_HEADS = 8
A_KV_HEADS = 2
A_GROUP = A_HEADS // A_KV_HEADS
Q_BLOCK = 128

NA_HEADS = 8
NA_KH = 8
NA_KW = 16

SSM_HEADS = 8
SSM_HEAD_DIM = 64
SSM_INNER = SSM_HEADS * SSM_HEAD_DIM
SSM_GROUPS = 2
SSM_STATE = 64
SSM_CONV = 5
SSM_CHUNK = 128
XBC_DIM = SSM_INNER + 2 * SSM_GROUPS * SSM_STATE
N_DIRS = 2

A_Q_DIM = A_HEADS * HEAD_DIM
A_KV_DIM = A_KV_HEADS * HEAD_DIM
NA_DIM = NA_HEADS * HEAD_DIM
N_BRANCH = 3
SPLIT_SIZES = (A_Q_DIM, A_KV_DIM, A_KV_DIM, NA_DIM, NA_DIM, NA_DIM, SSM_INNER, XBC_DIM, N_DIRS * SSM_HEADS, N_BRANCH * D_MODEL)
IN_COLS = sum(SPLIT_SIZES)

N_EXPERTS = 16
EC_CAPACITY = 2
EXPERT_FF = 2048

kernel_name = "hybrid_diffusion_backbone"


def rmsnorm(x, g):
    x32 = x.astype(jnp.float32)
    y = x32 * lax.rsqrt(jnp.mean(x32 * x32, axis=-1, keepdims=True) + NORM_EPS)
    return (y * g.astype(jnp.float32)).astype(x.dtype)


def modulate(u, shift, scale):
    return u * (1 + scale) + shift


def split_cols(w):
    offs = np.cumsum(SPLIT_SIZES)[:-1].tolist()
    return jnp.split(w, offs, axis=-1)


def axial_rope_tables(n_tokens, dtype):
    t = jnp.arange(n_tokens, dtype=jnp.int32)
    rows = (t // GRID_W).astype(jnp.float32)
    cols = (t % GRID_W).astype(jnp.float32)
    n_freq = HEAD_DIM // 4
    inv = ROPE_BASE ** (-jnp.arange(n_freq, dtype=jnp.float32) / n_freq)
    ar = rows[:, None] * inv[None, :]
    ac = cols[:, None] * inv[None, :]
    ang = jnp.concatenate([ar, ar, ac, ac], axis=-1)
    return jnp.cos(ang).astype(dtype)[:, None, :], jnp.sin(ang).astype(dtype)[:, None, :]


def apply_rope(x, cos, sin):
    xr = x.reshape(*x.shape[:-1], 2, 2, HEAD_DIM // 4)
    rot = jnp.stack([-xr[..., 1, :], xr[..., 0, :]], axis=-2).reshape(x.shape)
    return x * cos + rot * sin


def gqa_branch(qx, kx, vx, qc, kc, vc, q_norm, k_norm, with_ctx_out):
    b, s = qx.shape[0], qx.shape[1]
    l = qc.shape[1]
    scale = HEAD_DIM ** -0.5
    cos, sin = axial_rope_tables(s, qx.dtype)
    qx = apply_rope(rmsnorm(qx.reshape(b, s, A_HEADS, HEAD_DIM), q_norm), cos, sin)
    kx = apply_rope(rmsnorm(kx.reshape(b, s, A_KV_HEADS, HEAD_DIM), k_norm), cos, sin)
    kc = rmsnorm(kc.reshape(b, l, A_KV_HEADS, HEAD_DIM), k_norm)
    vx = vx.reshape(b, s, A_KV_HEADS, HEAD_DIM)
    vc = vc.reshape(b, l, A_KV_HEADS, HEAD_DIM)
    k_all = jnp.concatenate([kx, kc], axis=1)
    v_all = jnp.concatenate([vx, vc], axis=1)

    def attend(qb, k, v):
        sc = jnp.einsum('bqkgd,bskd->bkgqs', qb, k, preferred_element_type=jnp.float32) * scale
        p = jax.nn.softmax(sc, axis=-1).astype(v.dtype)
        return jnp.einsum('bkgqs,bskd->bqkgd', p, v)

    qb = qx.reshape(b, s // Q_BLOCK, Q_BLOCK, A_KV_HEADS, A_GROUP, HEAD_DIM).swapaxes(0, 1)
    ox = lax.map(lambda blk: attend(blk, k_all, v_all), qb)
    ox = ox.swapaxes(0, 1).reshape(b, s, A_Q_DIM)
    if not with_ctx_out:
        return ox, None
    qc = rmsnorm(qc.reshape(b, l, A_KV_HEADS, A_GROUP, HEAD_DIM), q_norm)
    oc = attend(qc, kc, vc).reshape(b, l, A_Q_DIM)
    return ox, oc


def na_branch(qx, kx, vx, qc, kc, vc, rel_bias, with_ctx_out):
    b, s = qx.shape[0], qx.shape[1]
    l = qc.shape[1]
    rows = s // GRID_W
    kh = min(NA_KH, rows)
    scale = HEAD_DIM ** -0.5
    qg = qx.reshape(b, rows, GRID_W, NA_HEADS, HEAD_DIM)
    kg = kx.reshape(b, rows, GRID_W, NA_HEADS, HEAD_DIM)
    vg = vx.reshape(b, rows, GRID_W, NA_HEADS, HEAD_DIM)
    kc = kc.reshape(b, l, NA_HEADS, HEAD_DIM)
    vc = vc.reshape(b, l, NA_HEADS, HEAD_DIM)
    col = jnp.arange(GRID_W)
    c0 = jnp.clip(col - NA_KW // 2, 0, GRID_W - NA_KW)
    col_idx = c0[:, None] + jnp.arange(NA_KW)[None, :]
    col_off = col_idx - col[:, None] + (NA_KW - 1)

    def row_block(r):
        r0 = jnp.clip(r - kh // 2, 0, rows - kh)
        q_r = lax.dynamic_index_in_dim(qg, r, axis=1, keepdims=False)
        k_band = lax.dynamic_slice_in_dim(kg, r0, kh, axis=1)
        v_band = lax.dynamic_slice_in_dim(vg, r0, kh, axis=1)
        k_win = k_band[:, :, col_idx]
        v_win = v_band[:, :, col_idx]
        row_off = r0 + jnp.arange(kh) - r + (NA_KH - 1)
        bias = rel_bias[:, row_off][:, :, col_off].transpose(0, 2, 1, 3)
        s_loc = jnp.einsum('bqhd,bkqwhd->bhqkw', q_r, k_win, preferred_element_type=jnp.float32) * scale + bias[None].astype(jnp.float32)
        s_ctx = jnp.einsum('bqhd,bchd->bhqc', q_r, kc, preferred_element_type=jnp.float32) * scale
        n_loc = kh * NA_KW
        s_all = jnp.concatenate([s_loc.reshape(b, NA_HEADS, GRID_W, n_loc), s_ctx], axis=-1)
        p = jax.nn.softmax(s_all, axis=-1).astype(vx.dtype)
        p_loc = p[..., :n_loc].reshape(b, NA_HEADS, GRID_W, kh, NA_KW)
        p_ctx = p[..., n_loc:]
        return jnp.einsum('bhqkw,bkqwhd->bqhd', p_loc, v_win) + jnp.einsum('bhqc,bchd->bqhd', p_ctx, vc)

    ox = lax.map(row_block, jnp.arange(rows))
    ox = ox.swapaxes(0, 1).reshape(b, s, NA_DIM)
    if not with_ctx_out:
        return ox, None
    qc = qc.reshape(b, l, NA_HEADS, HEAD_DIM)
    sc = jnp.einsum('bqhd,bchd->bhqc', qc, kc, preferred_element_type=jnp.float32) * scale
    pc = jax.nn.softmax(sc, axis=-1).astype(vc.dtype)
    oc = jnp.einsum('bhqc,bchd->bqhd', pc, vc).reshape(b, l, NA_DIM)
    return ox, oc


def dwconv_silu(u, w, bias):
    ch = u.shape[-1]
    out = lax.conv_general_dilated(u, w[:, None, :].astype(u.dtype), window_strides=(1,),
                                   padding=[(SSM_CONV // 2, SSM_CONV // 2)],
                                   dimension_numbers=('NWC', 'WIO', 'NWC'), feature_group_count=ch)
    return jax.nn.silu(out + bias)


def ssd_scan(xs, dt, a, bm, cm, h0):
    b, l, h, p = xs.shape
    n = bm.shape[-1]
    q = SSM_CHUNK
    nc = l // q
    xs = xs.astype(jnp.float32).reshape(b, nc, q, h, p)
    bm = bm.astype(jnp.float32).reshape(b, nc, q, h, n)
    cm = cm.astype(jnp.float32).reshape(b, nc, q, h, n)
    dt = dt.astype(jnp.float32).reshape(b, nc, q, h)
    a_cs = jnp.cumsum(dt * a, axis=2)
    tri = jnp.tril(jnp.ones((q, q), dtype=bool))
    seg = a_cs[:, :, :, None, :] - a_cs[:, :, None, :, :]
    decay = jnp.exp(jnp.where(tri[None, None, :, :, None], seg, -jnp.inf))
    xdt = xs * dt[..., None]
    gmat = jnp.einsum('bcihn,bcjhn->bcijh', cm, bm) * decay
    y_diag = jnp.einsum('bcijh,bcjhp->bcihp', gmat, xdt)
    end_decay = jnp.exp(a_cs[:, :, -1:, :] - a_cs)
    states = jnp.einsum('bcjhn,bcjhp->bchpn', bm * end_decay[..., None], xdt)
    chunk_decay = jnp.exp(a_cs[:, :, -1, :])

    def step(hs, inp):
        st, dc = inp
        return hs * dc[:, :, None, None] + st, hs

    h_last, h_in = lax.scan(step, h0.astype(jnp.float32), (jnp.moveaxis(states, 1, 0), jnp.moveaxis(chunk_decay, 1, 0)))
    h_in = jnp.moveaxis(h_in, 0, 1)
    y_off = jnp.einsum('bcihn,bchpn->bcihp', cm * jnp.exp(a_cs)[..., None], h_in)
    return (y_diag + y_off).reshape(b, l, h, p), h_last


def ssd_branch(zx, xbc_x, dtr_x, zc, xbc_c, dtr_c, conv_w, conv_b, a_log, dt_bias, d_skip, norm_g, with_ctx_out):
    def unpack(xbc, dt_raw):
        bsz, ln = xbc.shape[0], xbc.shape[1]
        u = dwconv_silu(xbc, conv_w, conv_b)
        xs, bm, cm = jnp.split(u, [SSM_INNER, SSM_INNER + SSM_GROUPS * SSM_STATE], axis=-1)
        xs = xs.reshape(bsz, ln, SSM_HEADS, SSM_HEAD_DIM)
        bm = jnp.repeat(bm.reshape(bsz, ln, SSM_GROUPS, SSM_STATE), SSM_HEADS // SSM_GROUPS, axis=2)
        cm = jnp.repeat(cm.reshape(bsz, ln, SSM_GROUPS, SSM_STATE), SSM_HEADS // SSM_GROUPS, axis=2)
        dt = jax.nn.softplus(dt_raw.astype(jnp.float32).reshape(bsz, ln, N_DIRS, SSM_HEADS) + dt_bias.astype(jnp.float32))
        return xs, bm, cm, dt

    xs_c, b_c, c_c, dt_c = unpack(xbc_c, dtr_c)
    xs_x, b_x, c_x, dt_x = unpack(xbc_x, dtr_x)
    a = -jnp.exp(a_log.astype(jnp.float32))
    bsz = xs_x.shape[0]
    h0 = jnp.zeros((bsz, SSM_HEADS, SSM_HEAD_DIM, SSM_STATE), jnp.float32)
    ident = lambda t: t
    flip = lambda t: jnp.flip(t, axis=1)
    y_x = None
    y_c = None
    for d in range(N_DIRS):
        orient = ident if d == 0 else flip
        d_d = d_skip[d].astype(jnp.float32)[:, None]
        yc_d, hc_d = ssd_scan(orient(xs_c), orient(dt_c[:, :, d]), a[d], orient(b_c), orient(c_c), h0)
        yx_d, _ = ssd_scan(orient(xs_x), orient(dt_x[:, :, d]), a[d], orient(b_x), orient(c_x), hc_d)
        term_x = orient(yx_d) + d_d * xs_x.astype(jnp.float32)
        term_c = orient(yc_d) + d_d * xs_c.astype(jnp.float32)
        y_x = term_x if y_x is None else y_x + term_x
        y_c = term_c if y_c is None else y_c + term_c

    def gate_norm(y, z):
        yz = y.reshape(*z.shape) * jax.nn.silu(z.astype(jnp.float32))
        return rmsnorm(yz, norm_g).astype(z.dtype)

    out_x = gate_norm(y_x, zx)
    if not with_ctx_out:
        return out_x, None
    return out_x, gate_norm(y_c, zc)


def hybrid_mixer(ux, uc, w_in, q_norm, k_norm, na_rel_bias, conv_w, conv_b, ssm_a_log, ssm_dt_bias, ssm_d, ssm_norm,
                 w_branch_a, w_branch_b, w_branch_c, w_out, with_ctx_out):
    ws = split_cols(w_in)
    px = [ux @ w for w in ws]
    pc = [uc @ w for w in ws]
    ya_x, ya_c = gqa_branch(px[0], px[1], px[2], pc[0], pc[1], pc[2], q_norm, k_norm, with_ctx_out)
    yb_x, yb_c = na_branch(px[3], px[4], px[5], pc[3], pc[4], pc[5], na_rel_bias, with_ctx_out)
    yc_x, yc_c = ssd_branch(px[6], px[7], px[8], pc[6], pc[7], pc[8], conv_w, conv_b,
                            ssm_a_log, ssm_dt_bias, ssm_d, ssm_norm, with_ctx_out)

    def merge(ya, yb, yc, gate_logits):
        g = jax.nn.sigmoid(gate_logits.astype(jnp.float32)).astype(ya.dtype)
        ga, gb, gc = jnp.split(g, N_BRANCH, axis=-1)
        m = ga * (ya @ w_branch_a) + gb * (yb @ w_branch_b) + gc * (yc @ w_branch_c)
        return m @ w_out

    out_x = merge(ya_x, yb_x, yc_x, px[9])
    if not with_ctx_out:
        return out_x, None
    return out_x, merge(ya_c, yb_c, yc_c, pc[9])


def expert_choice_ffn(u, w_router, w_gate, w_up, w_down):
    n_tok = u.shape[1]
    d = u.shape[-1]
    cap = EC_CAPACITY * n_tok // N_EXPERTS
    aff = jax.nn.softmax(jnp.einsum('btd,de->bte', u, w_router, preferred_element_type=jnp.float32), axis=-1)
    top_aff, top_idx = lax.top_k(jnp.swapaxes(aff, 1, 2), cap)
    xs = jax.vmap(lambda ub, ib: ub[ib])(u, top_idx)
    hid = jax.nn.silu(jnp.einsum('becd,edf->becf', xs, w_gate)) * jnp.einsum('becd,edf->becf', xs, w_up)
    ys = jnp.einsum('becf,efd->becd', hid, w_down) * top_aff[..., None].astype(u.dtype)

    def scatter(ib, yb):
        return jnp.zeros((n_tok, d), yb.dtype).at[ib.reshape(-1)].add(yb.reshape(-1, d))

    return jax.vmap(scatter)(top_idx, ys)


def setup_inputs(seed: int = 0) -> dict:
    key = jax.random.key(seed)
    ks = jax.random.split(key, 32)
    f32 = jnp.float32
    L = DEPTH
    D = D_MODEL

    def nrm(k, shape, scale):
        return jax.random.normal(k, shape, f32) * scale

    def gain(k, shape):
        return 1.0 + 0.05 * jax.random.normal(k, shape, f32)

    dt0 = jnp.exp(jax.random.uniform(ks[17], (L, N_DIRS, SSM_HEADS), f32, math.log(1e-3), math.log(1e-1)))
    return {
        'x': nrm(ks[0], (BATCH, SEQ, D), 1.0),
        'c': nrm(ks[1], (BATCH, D), 1.0),
        'ctx': nrm(ks[2], (BATCH, CTX_LEN, D), 1.0),
        'c_ctx': nrm(ks[3], (D,), 1.0),
        'w_ada': nrm(ks[4], (L, D, N_ADA * D), 0.5 * D ** -0.5),
        'b_ada': nrm(ks[5], (L, N_ADA * D), 0.02),
        'norm_mix_pre': gain(ks[6], (L, D)),
        'norm_mix_post': gain(ks[7], (L, D)),
        'norm_ffn_pre': gain(ks[8], (L, D)),
        'norm_ffn_post': gain(ks[9], (L, D)),
        'w_in': nrm(ks[10], (L, D, IN_COLS), D ** -0.5),
        'q_norm': gain(ks[11], (L, HEAD_DIM)),
        'k_norm': gain(ks[12], (L, HEAD_DIM)),
        'na_rel_bias': nrm(ks[13], (L, NA_HEADS, 2 * NA_KH - 1, 2 * NA_KW - 1), 0.1),
        'conv_w': nrm(ks[14], (L, SSM_CONV, XBC_DIM), SSM_CONV ** -0.5),
        'conv_b': nrm(ks[15], (L, XBC_DIM), 0.02),
        'ssm_a_log': jnp.log(jax.random.uniform(ks[16], (L, N_DIRS, SSM_HEADS), f32, 1.0, 16.0)),
        'ssm_dt_bias': dt0 + jnp.log(-jnp.expm1(-dt0)),
        'ssm_d': gain(ks[18], (L, N_DIRS, SSM_HEADS)),
        'ssm_norm': gain(ks[19], (L, SSM_INNER)),
        'w_branch_a': nrm(ks[20], (L, A_Q_DIM, D), A_Q_DIM ** -0.5),
        'w_branch_b': nrm(ks[21], (L, NA_DIM, D), NA_DIM ** -0.5),
        'w_branch_c': nrm(ks[22], (L, SSM_INNER, D), SSM_INNER ** -0.5),
        'w_out': nrm(ks[23], (L, D, D), D ** -0.5),
        'w_router': nrm(ks[24], (L, D, N_EXPERTS), D ** -0.5),
        'w_exp_gate': nrm(ks[25], (L, N_EXPERTS, D, EXPERT_FF), D ** -0.5),
        'w_exp_up': nrm(ks[26], (L, N_EXPERTS, D, EXPERT_FF), D ** -0.5),
        'w_exp_down': nrm(ks[27], (L, N_EXPERTS, EXPERT_FF, D), EXPERT_FF ** -0.5),
    }


def reference(x, c, ctx, c_ctx, w_ada, b_ada, norm_mix_pre, norm_mix_post, norm_ffn_pre, norm_ffn_post,
              w_in, q_norm, k_norm, na_rel_bias, conv_w, conv_b, ssm_a_log, ssm_dt_bias, ssm_d, ssm_norm,
              w_branch_a, w_branch_b, w_branch_c, w_out, w_router, w_exp_gate, w_exp_up, w_exp_down):
    hx = x
    hc = ctx
    for i in range(DEPTH):
        last = i == DEPTH - 1
        mod_x = (jax.nn.silu(c) @ w_ada[i] + b_ada[i])[:, None, :]
        mod_c = (jax.nn.silu(c_ctx) @ w_ada[i] + b_ada[i])[None, None, :]
        sh_m, sc_m, g_m, sh_f, sc_f, g_f = jnp.split(mod_x, N_ADA, axis=-1)
        csh_m, csc_m, cg_m, csh_f, csc_f, cg_f = jnp.split(mod_c, N_ADA, axis=-1)

        ux = modulate(rmsnorm(hx, norm_mix_pre[i]), sh_m, sc_m)
        uc = modulate(rmsnorm(hc, norm_mix_pre[i]), csh_m, csc_m)
        yx, yc = hybrid_mixer(ux, uc, w_in[i], q_norm[i], k_norm[i], na_rel_bias[i], conv_w[i], conv_b[i],
                              ssm_a_log[i], ssm_dt_bias[i], ssm_d[i], ssm_norm[i],
                              w_branch_a[i], w_branch_b[i], w_branch_c[i], w_out[i], not last)
        hx = hx + g_m * rmsnorm(yx, norm_mix_post[i])

        vx = modulate(rmsnorm(hx, norm_ffn_pre[i]), sh_f, sc_f)
        hx = hx + g_f * rmsnorm(expert_choice_ffn(vx, w_router[i], w_exp_gate[i], w_exp_up[i], w_exp_down[i]), norm_ffn_post[i])

        if not last:
            hc = hc + cg_m * rmsnorm(yc, norm_mix_post[i])
            vc = modulate(rmsnorm(hc, norm_ffn_pre[i]), csh_f, csc_f)
            hc = hc + cg_f * rmsnorm(expert_choice_ffn(vc, w_router[i], w_exp_gate[i], w_exp_up[i], w_exp_down[i]), norm_ffn_post[i])
    return hx
```

```python
import functools

import jax
import jax.numpy as jnp
import numpy as np
from jax import lax
from jax.experimental import pallas as pl
from jax.experimental.pallas import tpu as pltpu

F32, BF16, I32 = jnp.float32, jnp.bfloat16, jnp.int32
HIGHEST = lax.Precision.HIGHEST

GRID_W = 64
HEAD_DIM = 64
ROPE_BASE = 10000.0
NORM_EPS = 1e-6
N_ADA = 6
A_HEADS, A_KV_HEADS = 8, 2
NA_HEADS, NA_KH, NA_KW = 8, 8, 16
SSM_HEADS, SSM_HEAD_DIM, SSM_INNER = 8, 64, 512
SSM_GROUPS, SSM_STATE, SSM_CONV, N_DIRS = 2, 64, 5, 2
XBC_DIM = SSM_INNER + 2 * SSM_GROUPS * SSM_STATE
A_Q_DIM, A_KV_DIM, NA_DIM = A_HEADS * HEAD_DIM, A_KV_HEADS * HEAD_DIM, NA_HEADS * HEAD_DIM
N_EXPERTS, EC_CAPACITY = 16, 2
N_BRANCH = 3
ATT_SCALE = HEAD_DIM ** -0.5

V7X_LANES = 128
V7X_VMEM_LIMIT_BYTES = 56 * 1024 * 1024
NEG_BIG = -1e30

SSD_CHUNK = 128
ROUTE_TILE = 256
SLOT_BLOCK = 256


def _cparams(sem):
    return pltpu.CompilerParams(dimension_semantics=sem, vmem_limit_bytes=V7X_VMEM_LIMIT_BYTES)


def _nt_dot(a, b, precision=None):
    return lax.dot_general(a, b, (((1,), (1,)), ((), ())), preferred_element_type=F32,
                           precision=precision)


def _dot(a, b, precision=None):
    return jnp.dot(a, b, preferred_element_type=F32, precision=precision)


def _silu(x):
    return x * jax.nn.sigmoid(x)


def _rms(x, gain):
    return x * lax.rsqrt(jnp.mean(x * x, axis=-1, keepdims=True) + NORM_EPS) * gain


def _ada_kernel(c_ref, w_ref, b_ref, o_ref):
    o_ref[0] = _dot(_silu(c_ref[...]), w_ref[0], HIGHEST) + b_ref[0]


def _ada_mod(cc, w_ada, b_ada):
    nl, d, n = w_ada.shape
    tn = 512
    return pl.pallas_call(
        _ada_kernel,
        out_shape=jax.ShapeDtypeStruct((nl, 8, n), F32),
        grid=(nl, n // tn),
        in_specs=[pl.BlockSpec((8, d), lambda l, j: (0, 0)),
                  pl.BlockSpec((1, d, tn), lambda l, j: (l, 0, j)),
                  pl.BlockSpec((1, 1, tn), lambda l, j: (l, 0, j))],
        out_specs=pl.BlockSpec((1, 8, tn), lambda l, j: (l, 0, j)),
        compiler_params=_cparams(("parallel", "parallel")),
        name="ada_mod",
    )(cc, w_ada, b_ada.reshape(nl, 1, n))


_OFF_QA, _OFF_KA, _OFF_VA, _OFF_QB, _OFF_KB, _OFF_VB, _OFF_Z, _OFF_XBC, _OFF_G, _OFF_END = (
    0, 512, 640, 768, 1280, 1792, 2304, 2816, 3584, 6656)
_DT_COL = 3584


def _head_rms(x, hm, gain):
    ms = _dot((x * x).astype(BF16), hm)
    return x * lax.rsqrt(ms + NORM_EPS) * gain


def _rope(x, cos, sin):
    w = x.shape[-1]
    lane = lax.broadcasted_iota(I32, x.shape, 1)
    first = (lane & 16) == 0
    rot = jnp.where(first, -pltpu.roll(x, w - 16, 1), pltpu.roll(x, 16, 1))
    return x * cos + rot * sin


def _proj_kernel(rope, h_ref, g_ref, sh_ref, sc_ref, w_ref, wdt_ref, qn_ref, kn_ref, hm_ref, *rest):
    if rope:
        cos_ref, sin_ref = rest[:2]
        rest = rest[2:]
    qa_o, ka_o, va_o, qb_o, kb_o, vb_o, z_o, xbc_o, dt_o, gt_o = rest
    u = _rms(h_ref[0], g_ref[...]) * (1.0 + sc_ref[0]) + sh_ref[0]
    ub = u.astype(BF16)

    def mm(lo, hi):
        return _dot(ub, w_ref[:, lo:hi])

    qa = _head_rms(mm(_OFF_QA, _OFF_KA), hm_ref[...], qn_ref[...])
    ka = _head_rms(mm(_OFF_KA, _OFF_VA), hm_ref[:A_KV_DIM, :A_KV_DIM], kn_ref[...])
    if rope:
        cs, sn = cos_ref[...], sin_ref[...]
        qa = _rope(qa, jnp.concatenate([cs] * 4, axis=1), jnp.concatenate([sn] * 4, axis=1))
        ka = _rope(ka, cs, sn)
    qa_o[0] = (qa * ATT_SCALE).astype(BF16)
    ka_o[0] = ka.astype(BF16)
    va_o[0] = mm(_OFF_VA, _OFF_QB).astype(BF16)
    qb_o[0] = (mm(_OFF_QB, _OFF_KB) * ATT_SCALE).astype(BF16)
    kb_o[0] = mm(_OFF_KB, _OFF_VB).astype(BF16)
    vb_o[0] = mm(_OFF_VB, _OFF_Z).astype(BF16)
    z_o[0] = mm(_OFF_Z, _OFF_XBC)
    xbc_o[0] = mm(_OFF_XBC, _OFF_G)
    dt_o[0] = _dot(ub, wdt_ref[...])
    for i in range((_OFF_END - _OFF_G) // 512):
        gt_o[0, :, i * 512:(i + 1) * 512] = mm(_OFF_G + i * 512, _OFF_G + (i + 1) * 512)


def _proj(h, gain, shift, scale, lw, rope_tabs):
    b, t, d = h.shape
    tm = 256
    rope = rope_tabs is not None
    const = lambda shape: pl.BlockSpec(shape, lambda i, j: (0,) * len(shape))
    tok = lambda w: pl.BlockSpec((1, tm, w), lambda i, j: (i, j, 0))
    vec = pl.BlockSpec((1, 1, d), lambda i, j: (i, 0, 0))
    in_specs = [tok(d), const((1, d)), vec, vec, const((d, _OFF_END)), const((d, V7X_LANES)),
                const((1, A_Q_DIM)), const((1, A_KV_DIM)), const((A_Q_DIM, A_Q_DIM))]
    args = [h, gain, shift, scale, lw["w_main"], lw["w_dt"], lw["qn"], lw["kn"], lw["hm"]]
    if rope:
        tab = pl.BlockSpec((tm, V7X_LANES), lambda i, j: (j, 0))
        in_specs += [tab, tab]
        args += list(rope_tabs)
    widths = [(A_Q_DIM, BF16), (A_KV_DIM, BF16), (A_KV_DIM, BF16), (NA_DIM, BF16), (NA_DIM, BF16),
              (NA_DIM, BF16), (SSM_INNER, F32), (XBC_DIM, F32), (V7X_LANES, F32),
              (_OFF_END - _OFF_G, F32)]
    return pl.pallas_call(
        functools.partial(_proj_kernel, rope),
        out_shape=[jax.ShapeDtypeStruct((b, t, w), dt) for w, dt in widths],
        grid=(b, t // tm),
        in_specs=in_specs,
        out_specs=[tok(w) for w, _ in widths],
        compiler_params=_cparams(("parallel", "parallel")),
        name="norm_mod_proj",
    )(*args)


def _flash_kernel(nkv, grp, q_ref, k_ref, v_ref, o_ref, q_s, m_s, l_s, acc_s):
    j = pl.program_id(2)
    tq = q_ref.shape[1]
    hd = HEAD_DIM

    @pl.when(j == 0)
    def _():
        qb = q_ref[0]
        for g in range(nkv):
            parts = [qb[:, (g * grp + i) * hd:(g * grp + i + 1) * hd] for i in range(grp)]
            q_s[g] = parts[0] if grp == 1 else jnp.concatenate(parts, axis=0)
        m_s[...] = jnp.full(m_s.shape, -jnp.inf, F32)
        l_s[...] = jnp.zeros(l_s.shape, F32)
        acc_s[...] = jnp.zeros(acc_s.shape, F32)

    kb, vb = k_ref[0], v_ref[0]
    for g in range(nkv):
        s = _nt_dot(q_s[g], kb[:, g * hd:(g + 1) * hd])
        m_prev = m_s[g]
        m_new = jnp.maximum(m_prev, jnp.max(s, axis=-1, keepdims=True))
        alpha = jnp.exp(m_prev - m_new)
        p = jnp.exp(s - m_new)
        l_s[g] = alpha * l_s[g] + jnp.sum(p, axis=-1, keepdims=True)
        acc_s[g] = alpha * acc_s[g] + _dot(p.astype(BF16), vb[:, g * hd:(g + 1) * hd])
        m_s[g] = m_new

    @pl.when(j == pl.num_programs(2) - 1)
    def _():
        for g in range(nkv):
            o = acc_s[g] / l_s[g]
            for i in range(grp):
                c0 = (g * grp + i) * hd
                o_ref[0, :, c0:c0 + hd] = o[i * tq:(i + 1) * tq].astype(o_ref.dtype)


def _pick_tile(n, cands):
    for c in cands:
        if n % c == 0:
            return c
    raise ValueError(f"no tile for {n}")


def _flash(q, k, v, nkv, grp):
    b, s, qd = q.shape
    sk, kd = k.shape[1], k.shape[2]
    tq = _pick_tile(s, (256, 128))
    tk = _pick_tile(sk, (640, 512, 256, 128))
    rows = grp * tq
    return pl.pallas_call(
        functools.partial(_flash_kernel, nkv, grp),
        out_shape=jax.ShapeDtypeStruct((b, s, qd), BF16),
        grid=(b, s // tq, sk // tk),
        in_specs=[pl.BlockSpec((1, tq, qd), lambda i, m, j: (i, m, 0)),
                  pl.BlockSpec((1, tk, kd), lambda i, m, j: (i, j, 0)),
                  pl.BlockSpec((1, tk, kd), lambda i, m, j: (i, j, 0))],
        out_specs=pl.BlockSpec((1, tq, qd), lambda i, m, j: (i, m, 0)),
        scratch_shapes=[pltpu.VMEM((nkv, rows, HEAD_DIM), BF16),
                        pltpu.VMEM((nkv, rows, 1), F32),
                        pltpu.VMEM((nkv, rows, 1), F32),
                        pltpu.VMEM((nkv, rows, HEAD_DIM), F32)],
        compiler_params=_cparams(("parallel", "parallel", "arbitrary")),
        name=f"flash_attn_{nkv}x{grp}",
    )(q, k, v)


_NA_ROWS_PER_STEP = 8
_NA_BLOCK = _NA_ROWS_PER_STEP * GRID_W
_NA_BAND = NA_KH * GRID_W


def _na_kernel(rows, q_ref, kp_ref, kc_ref, kn_ref, vp_ref, vc_ref, vn_ref, kx_ref, vx_ref, bias_ref,
               o_ref, kcat_s, vcat_s):
    g = pl.program_id(1)
    nb = _NA_BLOCK
    for i, (kr, vr) in enumerate(((kp_ref, vp_ref), (kc_ref, vc_ref), (kn_ref, vn_ref))):
        kcat_s[i * nb:(i + 1) * nb, :] = kr[0]
        vcat_s[i * nb:(i + 1) * nb, :] = vr[0]
    kx, vx = kx_ref[0], vx_ref[0]
    hd = HEAD_DIM

    def body(j, carry):
        r = g * _NA_ROWS_PER_STEP + j
        r0 = jnp.clip(r - NA_KH // 2, 0, rows - NA_KH)
        off = pl.multiple_of((r0 - g * _NA_ROWS_PER_STEP + _NA_ROWS_PER_STEP) * GRID_W, GRID_W)
        case = r - r0
        qrow = q_ref[0, pl.ds(pl.multiple_of(j * GRID_W, GRID_W), GRID_W), :]
        kband = kcat_s[pl.ds(off, _NA_BAND), :]
        vband = vcat_s[pl.ds(off, _NA_BAND), :]
        outs = []
        for h in range(NA_HEADS):
            sl = slice(h * hd, (h + 1) * hd)
            q = qrow[:, sl]
            s_loc = _nt_dot(q, kband[:, sl]) + bias_ref[case * NA_HEADS + h]
            s_ctx = _nt_dot(q, kx[:, sl])
            m = jnp.maximum(jnp.max(s_loc, axis=-1, keepdims=True), jnp.max(s_ctx, axis=-1, keepdims=True))
            p_loc = jnp.exp(s_loc - m)
            p_ctx = jnp.exp(s_ctx - m)
            l = jnp.sum(p_loc, axis=-1, keepdims=True) + jnp.sum(p_ctx, axis=-1, keepdims=True)
            o = _dot(p_loc.astype(BF16), vband[:, sl]) + _dot(p_ctx.astype(BF16), vx[:, sl])
            outs.append(o / l)
        o_ref[0, pl.ds(pl.multiple_of(j * GRID_W, GRID_W), GRID_W), :] = (
            jnp.concatenate(outs, axis=1).astype(o_ref.dtype))
        return carry

    lax.fori_loop(0, _NA_ROWS_PER_STEP, body, 0)


def _na_bias_table(rel_bias):
    col = np.arange(GRID_W)
    c0 = np.clip(col - NA_KW // 2, 0, GRID_W - NA_KW)
    cc = np.arange(GRID_W)
    valid = (cc[None, :] >= c0[:, None]) & (cc[None, :] < c0[:, None] + NA_KW)
    col_off = np.clip(cc[None, :] - col[:, None] + (NA_KW - 1), 0, 2 * NA_KW - 2)
    case = np.arange(NA_KH)
    row_off = np.arange(NA_KH)[None, :] - case[:, None] + (NA_KH - 1)
    tab = rel_bias[:, row_off[:, :, None, None], col_off[None, None, :, :]]
    tab = jnp.where(valid[None, None, None], tab, NEG_BIG)
    tab = tab.transpose(1, 0, 3, 2, 4)
    return tab.reshape(NA_KH * NA_HEADS, GRID_W, NA_KH * GRID_W).astype(F32)


def _na(q, k, v, kx, vx, bias_tab):
    b, s, d = q.shape
    l = kx.shape[1]
    rows = s // GRID_W
    assert rows % _NA_ROWS_PER_STEP == 0 and rows >= NA_KH
    ng = rows // _NA_ROWS_PER_STEP
    nb = _NA_BLOCK
    cur = pl.BlockSpec((1, nb, d), lambda i, g: (i, g, 0))
    prv = pl.BlockSpec((1, nb, d), lambda i, g: (i, jnp.maximum(g - 1, 0), 0))
    nxt = pl.BlockSpec((1, nb, d), lambda i, g: (i, jnp.minimum(g + 1, ng - 1), 0))
    ctx = pl.BlockSpec((1, l, d), lambda i, g: (i, 0, 0))
    return pl.pallas_call(
        functools.partial(_na_kernel, rows),
        out_shape=jax.ShapeDtypeStruct((b, s, d), BF16),
        grid=(b, ng),
        in_specs=[cur, prv, cur, nxt, prv, cur, nxt, ctx, ctx,
                  pl.BlockSpec(bias_tab.shape, lambda i, g: (0, 0, 0))],
        out_specs=cur,
        scratch_shapes=[pltpu.VMEM((3 * nb, d), BF16), pltpu.VMEM((3 * nb, d), BF16)],
        compiler_params=_cparams(("parallel", "parallel")),
        name="neighbourhood_attn",
    )(q, k, k, k, v, v, v, kx, vx, bias_tab)


def _softplus(x):
    return jnp.maximum(x, 0.0) + jnp.log1p(jnp.exp(-jnp.abs(x)))


def _ssd_kernel(d, final, xc_ref, xp_ref, xn_ref, dt_ref, dtt_ref, cw_ref, cb_ref, dtb_ref, dtbt_ref,
                a_ref, at_ref, h0_ref, *rest):
    if final:
        y0_ref, z_ref, dsk_ref, ng_ref, y_o, h_o, h_s = rest
    else:
        y_o, h_o, h_s = rest
    t = pl.program_id(1)
    nc = pl.num_programs(1)
    reverse = d == 1
    c = nc - 1 - t if reverse else t
    q = xc_ref.shape[1]
    hp = SSM_HEAD_DIM
    ns = SSM_STATE

    @pl.when(t == 0)
    def _():
        h_s[...] = h0_ref[0]

    pm = (c > 0).astype(F32)
    nm = (c < nc - 1).astype(F32)
    xcat = jnp.concatenate([xp_ref[0] * pm, xc_ref[0], xn_ref[0] * nm], axis=0)
    n_cat = q + 16
    acc = jnp.zeros((q, XBC_DIM), F32)
    for k in range(SSM_CONV):
        sh = (SSM_CONV // 2 - k) % n_cat
        xs_k = xcat if sh == 0 else pltpu.roll(xcat, sh, 0)
        acc = acc + xs_k[8:8 + q] * cw_ref[k:k + 1, :]
    u = _silu(acc + cb_ref[...])
    xs = u[:, :SSM_INNER]
    bm = u[:, SSM_INNER:SSM_INNER + SSM_GROUPS * ns]
    cm = u[:, SSM_INNER + SSM_GROUPS * ns:]

    dt = _softplus(dt_ref[0] + dtb_ref[...])
    dtt = _softplus(dtt_ref[0] + dtbt_ref[...])
    ii = lax.broadcasted_iota(I32, (q, q), 0)
    jj = lax.broadcasted_iota(I32, (q, q), 1)
    keep = (jj >= ii) if reverse else (jj <= ii)
    tri = keep.astype(F32)
    acs = _dot(tri, dt * a_ref[...], HIGHEST)
    acst = _nt_dot(dtt * at_ref[...], tri, HIGHEST)
    last = 0 if reverse else q - 1

    xdt_all = jnp.concatenate(
        [xs[:, h * hp:(h + 1) * hp] * dt[:, d * SSM_HEADS + h:d * SSM_HEADS + h + 1]
         for h in range(SSM_HEADS)], axis=1)
    xdt_t = [xdt_all[:, p * V7X_LANES:(p + 1) * V7X_LANES].T for p in range(SSM_INNER // V7X_LANES)]
    xdt_b = xdt_all.astype(BF16)
    gmats = []
    for gi in range(SSM_GROUPS):
        gmats.append(_nt_dot(cm[:, gi * ns:(gi + 1) * ns].astype(BF16),
                             bm[:, gi * ns:(gi + 1) * ns].astype(BF16)))
    ys = []
    for h in range(SSM_HEADS):
        col = d * SSM_HEADS + h
        gi = h // (SSM_HEADS // SSM_GROUPS)
        a_col = acs[:, col:col + 1]
        a_row = acst[col:col + 1, :]
        tot = a_col[last:last + 1, :]
        lmat = jnp.exp(jnp.where(keep, a_col - a_row, NEG_BIG))
        y_diag = _dot((gmats[gi] * lmat).astype(BF16), xdt_b[:, h * hp:(h + 1) * hp])
        hst = h_s[h]
        cg = cm[:, gi * ns:(gi + 1) * ns]
        bg = bm[:, gi * ns:(gi + 1) * ns]
        y_off = _nt_dot((cg * jnp.exp(a_col)).astype(BF16), hst.astype(BF16))
        ys.append(y_diag + y_off)
        bdec = (bg * jnp.exp(tot - a_col)).astype(BF16)
        xt = xdt_t[h // 2][(h % 2) * hp:(h % 2 + 1) * hp, :]
        h_s[h] = jnp.exp(tot) * hst + _dot(xt.astype(BF16), bdec)
    y = jnp.concatenate(ys, axis=1)

    if final:
        yt = y0_ref[0] + y + dsk_ref[...] * xs
        y_o[0] = _rms(yt * _silu(z_ref[0]), ng_ref[...]).astype(y_o.dtype)
    else:
        y_o[0] = y

    @pl.when(t == nc - 1)
    def _():
        h_o[0] = h_s[...]


def _ssd_pass(d, xbc, dt, dtt, lw, h0, fin=None):
    b, l, _ = xbc.shape
    q = SSD_CHUNK
    nc = l // q
    final = fin is not None
    cidx = (lambda t: nc - 1 - t) if d == 1 else (lambda t: t)
    r8 = q // 8
    nb8 = l // 8
    const = lambda shape: pl.BlockSpec(shape, lambda i, t: (0,) * len(shape))
    tok = lambda w: pl.BlockSpec((1, q, w), lambda i, t: (i, cidx(t), 0))
    hspec = pl.BlockSpec((1, SSM_HEADS, SSM_HEAD_DIM, SSM_STATE), lambda i, t: (i, 0, 0, 0))
    in_specs = [
        tok(XBC_DIM),
        pl.BlockSpec((1, 8, XBC_DIM), lambda i, t: (i, jnp.maximum(cidx(t) * r8 - 1, 0), 0)),
        pl.BlockSpec((1, 8, XBC_DIM), lambda i, t: (i, jnp.minimum((cidx(t) + 1) * r8, nb8 - 1), 0)),
        tok(V7X_LANES),
        pl.BlockSpec((1, 16, q), lambda i, t: (i, 0, cidx(t))),
        const((8, XBC_DIM)), const((1, XBC_DIM)), const((1, V7X_LANES)), const((16, 1)),
        const((1, V7X_LANES)), const((16, 1)), hspec]
    args = [xbc, xbc, xbc, dt, dtt, lw["conv_w"], lw["conv_b"], lw["dt_b"], lw["dt_bt"],
            lw["a_row"], lw["a_col"], h0]
    if final:
        in_specs += [tok(SSM_INNER), tok(SSM_INNER), const((1, SSM_INNER)), const((1, SSM_INNER))]
        args += [fin[0], fin[1], lw["d_skip"], lw["ssm_norm"]]
    return pl.pallas_call(
        functools.partial(_ssd_kernel, d, final),
        out_shape=[jax.ShapeDtypeStruct((b, l, SSM_INNER), BF16 if final else F32),
                   jax.ShapeDtypeStruct(h0.shape, F32)],
        grid=(b, nc),
        in_specs=in_specs,
        out_specs=[tok(SSM_INNER), hspec],
        scratch_shapes=[pltpu.VMEM((SSM_HEADS, SSM_HEAD_DIM, SSM_STATE), F32)],
        compiler_params=_cparams(("parallel", "arbitrary")),
        name=f"ssd_scan_d{d}{'_final' if final else ''}",
    )(*args)


def _merge_kernel(ya_ref, yb_ref, yc_ref, gt_ref, hx_ref, wa_ref, wb_ref, wc_ref, wo_ref, gm_ref,
                  npost_ref, npre_ref, shf_ref, scf_ref, wr_ref, hx_o, vx_o, aff_o):
    d = hx_ref.shape[2]
    gt = gt_ref[0]
    m = (jax.nn.sigmoid(gt[:, :d]) * _dot(ya_ref[0], wa_ref[...])
         + jax.nn.sigmoid(gt[:, d:2 * d]) * _dot(yb_ref[0], wb_ref[...])
         + jax.nn.sigmoid(gt[:, 2 * d:]) * _dot(yc_ref[0], wc_ref[...]))
    out = _dot(m.astype(BF16), wo_ref[...])
    hx = hx_ref[0] + gm_ref[0] * _rms(out, npost_ref[...])
    hx_o[0] = hx
    v = _rms(hx, npre_ref[...]) * (1.0 + scf_ref[0]) + shf_ref[0]
    vx_o[0] = v.astype(BF16)
    logits = _nt_dot(wr_ref[...], v, HIGHEST)
    e = jnp.exp(logits - jnp.max(logits, axis=0, keepdims=True))
    aff_o[0] = e / jnp.sum(e, axis=0, keepdims=True)


def _merge(ya, yb, yc, gates, hx, lw, gm, shf, scf):
    b, t, d = hx.shape
    tm = 256
    const = lambda shape: pl.BlockSpec(shape, lambda i, j: (0,) * len(shape))
    tok = lambda w: pl.BlockSpec((1, tm, w), lambda i, j: (i, j, 0))
    vec = pl.BlockSpec((1, 1, d), lambda i, j: (i, 0, 0))
    return pl.pallas_call(
        _merge_kernel,
        out_shape=[jax.ShapeDtypeStruct((b, t, d), F32), jax.ShapeDtypeStruct((b, t, d), BF16),
                   jax.ShapeDtypeStruct((b, N_EXPERTS, t), F32)],
        grid=(b, t // tm),
        in_specs=[tok(A_Q_DIM), tok(NA_DIM), tok(SSM_INNER), tok(N_BRANCH * d), tok(d),
                  const((A_Q_DIM, d)), const((NA_DIM, d)), const((SSM_INNER, d)), const((d, d)),
                  vec, const((1, d)), const((1, d)), vec, vec, const((N_EXPERTS, d))],
        out_specs=[tok(d), tok(d), pl.BlockSpec((1, N_EXPERTS, tm), lambda i, j: (i, 0, j))],
        compiler_params=_cparams(("parallel", "parallel")),
        name="merge_residual_router",
    )(ya, yb, yc, gates, hx, lw["w_a"], lw["w_b"], lw["w_c"], lw["w_out"], gm, lw["n_post"],
      lw["n_ffn_pre"], shf, scf, lw["w_rt"])


def _topk_kernel(cap, aff_ref, pos_o, gate_o, cs_o, eq_s, gt_s, ceq_s, csel_s):
    x = aff_ref[0]
    ne, t = x.shape
    bits = pltpu.bitcast(x, I32)
    theta = jnp.zeros((ne, 1), I32)
    for bit in range(30, -1, -1):
        cand = theta | (1 << bit)
        cnt = jnp.sum((bits >= cand).astype(I32), axis=-1, keepdims=True)
        theta = jnp.where(cnt >= cap, cand, theta)
    gt = bits > theta
    eq = bits == theta
    need = cap - jnp.sum(gt.astype(I32), axis=-1, keepdims=True)
    eq_s[...] = eq.astype(F32)
    gt_s[...] = gt.astype(F32)
    tb = ROUTE_TILE
    ii = lax.broadcasted_iota(I32, (tb, tb), 0)
    jj = lax.broadcasted_iota(I32, (tb, tb), 1)
    upper = (ii <= jj).astype(BF16)
    lane = lax.broadcasted_iota(I32, (ne, V7X_LANES), 1)
    needf = need.astype(F32)

    ceq_s[...] = jnp.zeros(ceq_s.shape, F32)
    csel_s[...] = jnp.zeros(csel_s.shape, F32)
    cs_o[0] = jnp.zeros((ne, V7X_LANES), I32)

    def body(k, carry):
        c_eq, c_sel = ceq_s[...], csel_s[...]
        sl = pl.ds(pl.multiple_of(k * tb, tb), tb)
        e_k = eq_s[:, sl]
        g_k = gt_s[:, sl]
        inc_eq = _dot(e_k.astype(BF16), upper)
        rank_eq = c_eq + inc_eq - e_k
        sel = g_k + e_k * (rank_eq < needf).astype(F32)
        inc_sel = _dot(sel.astype(BF16), upper)
        slot = c_sel + inc_sel - sel
        pos_o[0, :, sl] = jnp.where(sel > 0.5, slot, -1.0).astype(I32)
        gate_o[0, :, sl] = jnp.where(sel > 0.5, aff_ref[0, :, sl], 0.0)
        cs_o[0] = jnp.where(lane == k, c_sel.astype(I32), cs_o[0])
        ceq_s[...] = c_eq + inc_eq[:, tb - 1:tb]
        csel_s[...] = c_sel + inc_sel[:, tb - 1:tb]
        return carry

    nk = t // tb
    lax.fori_loop(0, nk, body, 0)
    cs_o[0] = jnp.where(lane == nk, csel_s[...].astype(I32), cs_o[0])


def _topk(aff_t, cap):
    b, ne, t = aff_t.shape
    assert t % ROUTE_TILE == 0 and t // ROUTE_TILE < V7X_LANES
    full = pl.BlockSpec((1, ne, t), lambda i: (i, 0, 0))
    return pl.pallas_call(
        functools.partial(_topk_kernel, cap),
        out_shape=[jax.ShapeDtypeStruct((b, ne, t), I32), jax.ShapeDtypeStruct((b, ne, t), F32),
                   jax.ShapeDtypeStruct((b, ne, V7X_LANES), I32)],
        grid=(b,),
        in_specs=[full],
        out_specs=[full, full, pl.BlockSpec((1, ne, V7X_LANES), lambda i: (i, 0, 0))],
        scratch_shapes=[pltpu.VMEM((ne, t), F32), pltpu.VMEM((ne, t), F32),
                        pltpu.VMEM((ne, 1), F32), pltpu.VMEM((ne, 1), F32)],
        compiler_params=_cparams(("parallel",)),
        name="expert_topk",
    )(aff_t)


def _cs_at(cs_ref, b, e, k):
    return cs_ref[(b * N_EXPERTS + e) * V7X_LANES + k]


def _gather_kernel(sb, nblk, cs_ref, pos_ref, vx_ref, xs_o, cur_s, nxt_s):
    b, e, k = pl.program_id(0), pl.program_id(1), pl.program_id(2)
    blk = jnp.minimum(_cs_at(cs_ref, b, e, k) // sb, nblk - 1)
    blk_prev = jnp.minimum(_cs_at(cs_ref, b, e, jnp.maximum(k - 1, 0)) // sb, nblk - 1)

    @pl.when(k == 0)
    def _():
        cur_s[...] = jnp.zeros(cur_s.shape, F32)
        nxt_s[...] = jnp.zeros(nxt_s.shape, F32)

    @pl.when(jnp.logical_and(k > 0, blk != blk_prev))
    def _():
        cur_s[...] = nxt_s[...]
        nxt_s[...] = jnp.zeros(nxt_s.shape, F32)

    tt = vx_ref.shape[1]
    local = pos_ref[0, 0] - blk * sb
    onehot = (lax.broadcasted_iota(I32, (2 * sb, tt), 0) == local).astype(BF16)
    rows = _dot(onehot, vx_ref[0])
    cur_s[...] += rows[:sb]
    nxt_s[...] += rows[sb:]
    xs_o[0, 0] = cur_s[...].astype(xs_o.dtype)


def _gather(cs_flat, pos_t, vx, cap):
    b, t, d = vx.shape
    ne = N_EXPERTS
    tt = ROUTE_TILE
    sb = min(SLOT_BLOCK, cap)
    nblk = cap // sb
    blk_of = lambda i, e, k, cs: jnp.minimum(cs[(i * ne + e) * V7X_LANES + k] // sb, nblk - 1)
    grid_spec = pltpu.PrefetchScalarGridSpec(
        num_scalar_prefetch=1,
        grid=(b, ne, t // tt),
        in_specs=[pl.BlockSpec((1, 1, 1, tt), lambda i, e, k, cs: (i, e, 0, k)),
                  pl.BlockSpec((1, tt, d), lambda i, e, k, cs: (i, k, 0))],
        out_specs=pl.BlockSpec((1, 1, sb, d), lambda i, e, k, cs: (i, e, blk_of(i, e, k, cs), 0)),
        scratch_shapes=[pltpu.VMEM((sb, d), F32), pltpu.VMEM((sb, d), F32)])
    return pl.pallas_call(
        functools.partial(_gather_kernel, sb, nblk),
        out_shape=jax.ShapeDtypeStruct((b, ne, cap, d), BF16),
        grid_spec=grid_spec,
        compiler_params=_cparams(("parallel", "parallel", "arbitrary")),
        name="expert_gather",
    )(cs_flat, pos_t.reshape(b, ne, 1, t), vx)


def _ffn_kernel(x_ref, wg_ref, wu_ref, wd_ref, y_o):
    x = x_ref[0, 0]
    f = wg_ref.shape[2]
    tf = 512
    acc = jnp.zeros((x.shape[0], wd_ref.shape[2]), F32)
    for i in range(f // tf):
        sl = slice(i * tf, (i + 1) * tf)
        hid = _silu(_dot(x, wg_ref[0, :, sl])) * _dot(x, wu_ref[0, :, sl])
        acc = acc + _dot(hid.astype(BF16), wd_ref[0, sl, :])
    y_o[0, 0] = acc.astype(y_o.dtype)


def _expert_ffn(xs, wg, wu, wd):
    b, ne, cap, d = xs.shape
    f = wg.shape[2]
    tc = min(512, cap)
    tok = pl.BlockSpec((1, 1, tc, d), lambda e, i, c: (i, e, c, 0))
    return pl.pallas_call(
        _ffn_kernel,
        out_shape=jax.ShapeDtypeStruct(xs.shape, BF16),
        grid=(ne, b, cap // tc),
        in_specs=[tok,
                  pl.BlockSpec((1, d, f), lambda e, i, c: (e, 0, 0)),
                  pl.BlockSpec((1, d, f), lambda e, i, c: (e, 0, 0)),
                  pl.BlockSpec((1, f, d), lambda e, i, c: (e, 0, 0))],
        out_specs=tok,
        compiler_params=_cparams(("parallel", "parallel", "parallel")),
        name="expert_ffn",
    )(xs, wg, wu, wd)


def _combine_kernel(sb, nblk, cs_ref, y0_ref, y1_ref, pos_ref, gate_ref, hx_ref, gf_ref, npost_ref,
                    hx_o, acc_s):
    b, k, e = pl.program_id(0), pl.program_id(1), pl.program_id(2)
    blk = jnp.minimum(_cs_at(cs_ref, b, e, k) // sb, nblk - 1)

    @pl.when(e == 0)
    def _():
        acc_s[...] = jnp.zeros(acc_s.shape, F32)

    tt = hx_ref.shape[1]
    lane = lax.broadcasted_iota(I32, (tt, N_EXPERTS), 1)
    mine = lane == e
    slot = jnp.sum(jnp.where(mine, pos_ref[0], 0), axis=-1, keepdims=True)
    gate = jnp.sum(jnp.where(mine, gate_ref[0], 0.0), axis=-1, keepdims=True)
    local = slot - blk * sb
    onehot = (lax.broadcasted_iota(I32, (tt, 2 * sb), 1) == local).astype(BF16)
    ycat = jnp.concatenate([y0_ref[0, 0], y1_ref[0, 0]], axis=0)
    acc_s[...] += gate * _dot(onehot, ycat)

    @pl.when(e == pl.num_programs(2) - 1)
    def _():
        hx_o[0] = hx_ref[0] + gf_ref[0] * _rms(acc_s[...], npost_ref[...])


def _combine(cs_flat, ys, pos_r, gate_r, hx, gf, n_post):
    b, t, d = hx.shape
    ne, cap = ys.shape[1], ys.shape[2]
    tt = ROUTE_TILE
    sb = min(SLOT_BLOCK, cap)
    nblk = cap // sb
    blk_of = lambda i, k, e, cs: jnp.minimum(cs[(i * ne + e) * V7X_LANES + k] // sb, nblk - 1)
    tok = lambda w: pl.BlockSpec((1, tt, w), lambda i, k, e, cs: (i, k, 0))
    grid_spec = pltpu.PrefetchScalarGridSpec(
        num_scalar_prefetch=1,
        grid=(b, t // tt, ne),
        in_specs=[pl.BlockSpec((1, 1, sb, d), lambda i, k, e, cs: (i, e, blk_of(i, k, e, cs), 0)),
                  pl.BlockSpec((1, 1, sb, d),
                               lambda i, k, e, cs: (i, e, jnp.minimum(blk_of(i, k, e, cs) + 1, nblk - 1), 0)),
                  tok(ne), tok(ne), tok(d),
                  pl.BlockSpec((1, 1, d), lambda i, k, e, cs: (i, 0, 0)),
                  pl.BlockSpec((1, d), lambda i, k, e, cs: (0, 0))],
        out_specs=tok(d),
        scratch_shapes=[pltpu.VMEM((tt, d), F32)])
    return pl.pallas_call(
        functools.partial(_combine_kernel, sb, nblk),
        out_shape=jax.ShapeDtypeStruct((b, t, d), F32),
        grid_spec=grid_spec,
        compiler_params=_cparams(("parallel", "parallel", "arbitrary")),
        name="expert_combine",
    )(cs_flat, ys, ys, pos_r, gate_r, hx, gf, n_post)


def _moe(vx, aff_t, hx, gf, lw):
    b, t, d = hx.shape
    cap = EC_CAPACITY * t // N_EXPERTS
    pos_t, gate_t, cs = _topk(aff_t, cap)
    cs_flat = cs.reshape(-1)
    xs = _gather(cs_flat, pos_t, vx, cap)
    ys = _expert_ffn(xs, lw["w_gate"], lw["w_up"], lw["w_down"])
    return _combine(cs_flat, ys, pos_t.transpose(0, 2, 1), gate_t.transpose(0, 2, 1), hx, gf,
                    lw["n_ffn_post"])


def _rope_tables(s):
    t = jnp.arange(s, dtype=I32)
    rows = (t // GRID_W).astype(F32)
    cols = (t % GRID_W).astype(F32)
    n_freq = HEAD_DIM // 4
    inv = ROPE_BASE ** (-jnp.arange(n_freq, dtype=F32) / n_freq)
    ar = rows[:, None] * inv[None, :]
    ac = cols[:, None] * inv[None, :]
    ang = jnp.concatenate([ar, ar, ac, ac] * (V7X_LANES // HEAD_DIM), axis=-1)
    return jnp.cos(ang), jnp.sin(ang)


def _layer_weights(i, p):
    d = p["w_in"].shape[1]
    w_in = p["w_in"][i]
    dt_pad = jnp.zeros((d, V7X_LANES - N_DIRS * SSM_HEADS), F32)
    hm = np.kron(np.eye(A_HEADS, dtype=np.float32), np.full((HEAD_DIM, HEAD_DIM), 1.0 / HEAD_DIM, np.float32))
    lane_pad = lambda v: jnp.pad(v.reshape(1, -1), ((0, 0), (0, V7X_LANES - v.size)))
    a = -jnp.exp(p["ssm_a_log"][i].astype(F32)).reshape(-1)
    return {
        "w_main": jnp.concatenate([w_in[:, :_DT_COL], w_in[:, _DT_COL + N_DIRS * SSM_HEADS:]], axis=1).astype(BF16),
        "w_dt": jnp.concatenate([w_in[:, _DT_COL:_DT_COL + N_DIRS * SSM_HEADS], dt_pad], axis=1).astype(BF16),
        "qn": jnp.tile(p["q_norm"][i], A_HEADS).reshape(1, -1),
        "kn": jnp.tile(p["k_norm"][i], A_KV_HEADS).reshape(1, -1),
        "hm": jnp.asarray(hm, BF16),
        "n_pre": p["norm_mix_pre"][i].reshape(1, d),
        "n_post": p["norm_mix_post"][i].reshape(1, d),
        "n_ffn_pre": p["norm_ffn_pre"][i].reshape(1, d),
        "n_ffn_post": p["norm_ffn_post"][i].reshape(1, d),
        "bias_tab": _na_bias_table(p["na_rel_bias"][i]),
        "conv_w": jnp.pad(p["conv_w"][i], ((0, 8 - SSM_CONV), (0, 0))),
        "conv_b": p["conv_b"][i].reshape(1, -1),
        "dt_b": lane_pad(p["ssm_dt_bias"][i]),
        "dt_bt": p["ssm_dt_bias"][i].reshape(-1, 1),
        "a_row": lane_pad(a),
        "a_col": a.reshape(-1, 1),
        "d_skip": jnp.repeat(p["ssm_d"][i, 0] + p["ssm_d"][i, 1], SSM_HEAD_DIM).reshape(1, -1),
        "ssm_norm": p["ssm_norm"][i].reshape(1, -1),
        "w_a": p["w_branch_a"][i].astype(BF16),
        "w_b": p["w_branch_b"][i].astype(BF16),
        "w_c": p["w_branch_c"][i].astype(BF16),
        "w_out": p["w_out"][i].astype(BF16),
        "w_rt": p["w_router"][i].T,
        "w_gate": p["w_exp_gate"][i].astype(BF16),
        "w_up": p["w_exp_up"][i].astype(BF16),
        "w_down": p["w_exp_down"][i].astype(BF16),
    }


def _ssd_branch(px, pc, lw, with_ctx_out):
    zx, xbcx, dtx = px
    zc, xbcc, dtc = pc
    b = zx.shape[0]
    nd = N_DIRS * SSM_HEADS
    dttx = dtx[:, :, :nd].transpose(0, 2, 1)
    dttc = dtc[:, :, :nd].transpose(0, 2, 1)
    h0 = jnp.zeros((b, SSM_HEADS, SSM_HEAD_DIM, SSM_STATE), F32)
    y0c, hc0 = _ssd_pass(0, xbcc, dtc, dttc, lw, h0)
    yc, hc1 = _ssd_pass(1, xbcc, dtc, dttc, lw, h0, fin=(y0c, zc))
    y0x, _ = _ssd_pass(0, xbcx, dtx, dttx, lw, hc0)
    yx, _ = _ssd_pass(1, xbcx, dtx, dttx, lw, hc1, fin=(y0x, zx))
    return yx, (yc if with_ctx_out else None)


def _layer(i, hx, hc, mod_x, mod_c, p, rope_tabs, last):
    lw = _layer_weights(i, p)
    d = hx.shape[2]
    sh_m, sc_m, g_m, sh_f, sc_f, g_f = [mod_x[:, None, j * d:(j + 1) * d] for j in range(N_ADA)]
    csh_m, csc_m, cg_m, csh_f, csc_f, cg_f = [mod_c[:, None, j * d:(j + 1) * d] for j in range(N_ADA)]

    qa_c, ka_c, va_c, qb_c, kb_c, vb_c, z_c, xbc_c, dt_c, gt_c = _proj(hc, lw["n_pre"], csh_m, csc_m, lw, None)
    qa, ka, va, qb, kb, vb, z, xbc, dt, gt = _proj(hx, lw["n_pre"], sh_m, sc_m, lw, rope_tabs)

    ya = _flash(qa, jnp.concatenate([ka, ka_c], axis=1), jnp.concatenate([va, va_c], axis=1),
                A_KV_HEADS, A_HEADS // A_KV_HEADS)
    yb = _na(qb, kb, vb, kb_c, vb_c, lw["bias_tab"])
    yc, yc_c = _ssd_branch((z, xbc, dt), (z_c, xbc_c, dt_c), lw, not last)
    hx, vx, aff = _merge(ya, yb, yc, gt, hx, lw, g_m, sh_f, sc_f)
    hx = _moe(vx, aff, hx, g_f, lw)
    if last:
        return hx, hc
    ya_c = _flash(qa_c, ka_c, va_c, A_KV_HEADS, A_HEADS // A_KV_HEADS)
    yb_c = _flash(qb_c, kb_c, vb_c, NA_HEADS, 1)
    hc, vc, aff_c = _merge(ya_c, yb_c, yc_c, gt_c, hc, lw, cg_m, csh_f, csc_f)
    hc = _moe(vc, aff_c, hc, cg_f, lw)
    return hx, hc


def kernel(x, c, ctx, c_ctx, w_ada, b_ada, norm_mix_pre, norm_mix_post, norm_ffn_pre, norm_ffn_post, w_in, q_norm, k_norm, na_rel_bias, conv_w, conv_b, ssm_a_log, ssm_dt_bias, ssm_d, ssm_norm, w_branch_a, w_branch_b, w_branch_c, w_out, w_router, w_exp_gate, w_exp_up, w_exp_down):
    p = dict(norm_mix_pre=norm_mix_pre, norm_mix_post=norm_mix_post, norm_ffn_pre=norm_ffn_pre,
             norm_ffn_post=norm_ffn_post, w_in=w_in, q_norm=q_norm, k_norm=k_norm, na_rel_bias=na_rel_bias,
             conv_w=conv_w, conv_b=conv_b, ssm_a_log=ssm_a_log, ssm_dt_bias=ssm_dt_bias, ssm_d=ssm_d,
             ssm_norm=ssm_norm, w_branch_a=w_branch_a, w_branch_b=w_branch_b, w_branch_c=w_branch_c,
             w_out=w_out, w_router=w_router, w_exp_gate=w_exp_gate, w_exp_up=w_exp_up, w_exp_down=w_exp_down)
    b, s, d = x.shape
    depth = w_ada.shape[0]
    assert b + 1 <= 8
    cc = jnp.concatenate([c, c_ctx[None, :], jnp.zeros((8 - b - 1, d), F32)], axis=0)
    mod = _ada_mod(cc, w_ada, b_ada)
    rope_tabs = _rope_tables(s)
    hx, hc = x, ctx
    for i in range(depth):
        mod_x = mod[i, :b]
        mod_c = jnp.broadcast_to(mod[i, b:b + 1], (b, mod.shape[2]))
        hx, hc = _layer(i, hx, hc, mod_x, mod_c, p, rope_tabs, i == depth - 1)
    return hx
```

```python
import functools

import jax
import jax.numpy as jnp
import numpy as np
from jax import lax
from jax.experimental import pallas as pl
from jax.experimental.pallas import tpu as pltpu

F32, BF16, I32 = jnp.float32, jnp.bfloat16, jnp.int32
HIGHEST = lax.Precision.HIGHEST

GRID_W = 64
HEAD_DIM = 64
ROPE_BASE = 10000.0
NORM_EPS = 1e-6
N_ADA = 6
A_HEADS, A_KV_HEADS = 8, 2
NA_HEADS, NA_KH, NA_KW = 8, 8, 16
SSM_HEADS, SSM_HEAD_DIM, SSM_INNER = 8, 64, 512
SSM_GROUPS, SSM_STATE, SSM_CONV, N_DIRS = 2, 64, 5, 2
XBC_DIM = SSM_INNER + 2 * SSM_GROUPS * SSM_STATE
A_Q_DIM, A_KV_DIM, NA_DIM = A_HEADS * HEAD_DIM, A_KV_HEADS * HEAD_DIM, NA_HEADS * HEAD_DIM
N_EXPERTS, EC_CAPACITY = 16, 2
N_BRANCH = 3
ATT_SCALE = HEAD_DIM ** -0.5
LOG2_E = 1.4426950408889634

V7X_LANES = 128
V7X_VMEM_LIMIT_BYTES = 56 * 1024 * 1024
NEG_BIG = -1e30

SSD_CHUNK = 128
ROUTE_TILE = 256
SLOT_BLOCK = 256


def _cparams(sem):
    return pltpu.CompilerParams(dimension_semantics=sem, vmem_limit_bytes=V7X_VMEM_LIMIT_BYTES)


def _nt_dot(a, b, precision=None):
    return lax.dot_general(a, b, (((1,), (1,)), ((), ())), preferred_element_type=F32,
                           precision=precision)


def _dot(a, b, precision=None):
    return jnp.dot(a, b, preferred_element_type=F32, precision=precision)


def _silu(x):
    return x * jax.nn.sigmoid(x)


def _rms(x, gain):
    return x * lax.rsqrt(jnp.mean(x * x, axis=-1, keepdims=True) + NORM_EPS) * gain


def _ada_kernel(c_ref, w_ref, b_ref, o_ref):
    o_ref[0] = _dot(_silu(c_ref[...]), w_ref[0], HIGHEST) + b_ref[0]


def _ada_mod(cc, w_ada, b_ada):
    nl, d, n = w_ada.shape
    tn = 512
    return pl.pallas_call(
        _ada_kernel,
        out_shape=jax.ShapeDtypeStruct((nl, 8, n), F32),
        grid=(nl, n // tn),
        in_specs=[pl.BlockSpec((8, d), lambda l, j: (0, 0)),
                  pl.BlockSpec((1, d, tn), lambda l, j: (l, 0, j)),
                  pl.BlockSpec((1, 1, tn), lambda l, j: (l, 0, j))],
        out_specs=pl.BlockSpec((1, 8, tn), lambda l, j: (l, 0, j)),
        compiler_params=_cparams(("parallel", "parallel")),
        name="ada_mod",
    )(cc, w_ada, b_ada.reshape(nl, 1, n))


_OFF_QA, _OFF_KA, _OFF_VA, _OFF_QB, _OFF_KB, _OFF_VB, _OFF_Z, _OFF_XBC, _OFF_G, _OFF_END = (
    0, 512, 640, 768, 1280, 1792, 2304, 2816, 3584, 6656)
_DT_COL = 3584


def _head_rms(x, hm, gain):
    ms = _dot((x * x).astype(BF16), hm)
    return x * lax.rsqrt(ms + NORM_EPS) * gain


def _rope(x, cos, sin):
    w = x.shape[-1]
    lane = lax.broadcasted_iota(I32, x.shape, 1)
    first = (lane & 16) == 0
    rot = jnp.where(first, -pltpu.roll(x, w - 16, 1), pltpu.roll(x, 16, 1))
    return x * cos + rot * sin


def _proj_kernel(rope, h_ref, g_ref, sh_ref, sc_ref, w_ref, wdt_ref, qn_ref, kn_ref, hm_ref, *rest):
    if rope:
        cos_ref, sin_ref = rest[:2]
        rest = rest[2:]
    qa_o, ka_o, va_o, qb_o, kb_o, vb_o, z_o, xbc_o, dt_o, gt_o = rest
    u = _rms(h_ref[0], g_ref[...]) * (1.0 + sc_ref[0]) + sh_ref[0]
    ub = u.astype(BF16)

    def mm(lo, hi):
        return _dot(ub, w_ref[:, lo:hi])

    qa = _head_rms(mm(_OFF_QA, _OFF_KA), hm_ref[...], qn_ref[...])
    ka = _head_rms(mm(_OFF_KA, _OFF_VA), hm_ref[:A_KV_DIM, :A_KV_DIM], kn_ref[...])
    if rope:
        cs, sn = cos_ref[...], sin_ref[...]
        qa = _rope(qa, jnp.concatenate([cs] * 4, axis=1), jnp.concatenate([sn] * 4, axis=1))
        ka = _rope(ka, cs, sn)
    qa_o[0] = (qa * (ATT_SCALE * LOG2_E)).astype(BF16)
    ka_o[0] = ka.astype(BF16)
    va_o[0] = mm(_OFF_VA, _OFF_QB).astype(BF16)
    qb_o[0] = (mm(_OFF_QB, _OFF_KB) * ATT_SCALE).astype(BF16)
    kb_o[0] = mm(_OFF_KB, _OFF_VB).astype(BF16)
    vb_o[0] = mm(_OFF_VB, _OFF_Z).astype(BF16)
    z_o[0] = mm(_OFF_Z, _OFF_XBC)
    xbc_o[0] = mm(_OFF_XBC, _OFF_G)
    dt_o[0] = _dot(ub, wdt_ref[...])
    for i in range((_OFF_END - _OFF_G) // 512):
        gt_o[0, :, i * 512:(i + 1) * 512] = mm(_OFF_G + i * 512, _OFF_G + (i + 1) * 512)


def _proj(h, gain, shift, scale, lw, rope_tabs):
    b, t, d = h.shape
    tm = 256
    rope = rope_tabs is not None
    const = lambda shape: pl.BlockSpec(shape, lambda i, j: (0,) * len(shape))
    tok = lambda w: pl.BlockSpec((1, tm, w), lambda i, j: (i, j, 0))
    vec = pl.BlockSpec((1, 1, d), lambda i, j: (i, 0, 0))
    in_specs = [tok(d), const((1, d)), vec, vec, const((d, _OFF_END)), const((d, V7X_LANES)),
                const((1, A_Q_DIM)), const((1, A_KV_DIM)), const((A_Q_DIM, A_Q_DIM))]
    args = [h, gain, shift, scale, lw["w_main"], lw["w_dt"], lw["qn"], lw["kn"], lw["hm"]]
    if rope:
        tab = pl.BlockSpec((tm, V7X_LANES), lambda i, j: (j, 0))
        in_specs += [tab, tab]
        args += list(rope_tabs)
    widths = [(A_Q_DIM, BF16), (A_KV_DIM, BF16), (A_KV_DIM, BF16), (NA_DIM, BF16), (NA_DIM, BF16),
              (NA_DIM, BF16), (SSM_INNER, F32), (XBC_DIM, F32), (V7X_LANES, F32),
              (_OFF_END - _OFF_G, F32)]
    return pl.pallas_call(
        functools.partial(_proj_kernel, rope),
        out_shape=[jax.ShapeDtypeStruct((b, t, w), dt) for w, dt in widths],
        grid=(b, t // tm),
        in_specs=in_specs,
        out_specs=[tok(w) for w, _ in widths],
        compiler_params=_cparams(("parallel", "parallel")),
        name="norm_mod_proj",
    )(*args)


def _flash_kernel(nkv, grp, base2, q_ref, kt_ref, v_ref, o_ref, q_s, m_s, acc_s):
    j = pl.program_id(2)
    tq = q_ref.shape[1]
    tk = kt_ref.shape[2]
    hd = HEAD_DIM
    ex = jnp.exp2 if base2 else jnp.exp

    @pl.when(j == 0)
    def _():
        qb = q_ref[0]
        for g in range(nkv):
            parts = [qb[:, (g * grp + i) * hd:(g * grp + i + 1) * hd] for i in range(grp)]
            q_s[g] = parts[0] if grp == 1 else jnp.concatenate(parts, axis=0)
        m_s[...] = jnp.full(m_s.shape, -jnp.inf, F32)
        acc_s[...] = jnp.zeros(acc_s.shape, F32)

    for g in range(nkv):
        s = _dot(q_s[g], kt_ref[0, g * hd:(g + 1) * hd, :])
        m_prev = m_s[g]
        m_new = jnp.maximum(m_prev, jnp.max(s, axis=-1, keepdims=True))
        alpha = ex(m_prev - m_new)
        m_s[g] = m_new
        p = jnp.concatenate(
            [ex(s[:, c * V7X_LANES:(c + 1) * V7X_LANES] - m_new).astype(BF16) for c in range(tk // V7X_LANES)],
            axis=1)
        acc_s[g] = alpha * acc_s[g] + _dot(p, v_ref[0, :, g * V7X_LANES:(g + 1) * V7X_LANES])

    @pl.when(j == pl.num_programs(2) - 1)
    def _():
        for g in range(nkv):
            acc = acc_s[g]
            o = acc / pltpu.roll(acc, hd, 1)
            for i in range(grp):
                c0 = (g * grp + i) * hd
                o_ref[0, :, c0:c0 + hd] = o[i * tq:(i + 1) * tq, :hd].astype(o_ref.dtype)


def _pick_tile(n, cands):
    for c in cands:
        if n % c == 0:
            return c
    raise ValueError(f"no tile for {n}")


def _flash(q, k, v, nkv, grp, base2):
    b, s, qd = q.shape
    sk = k.shape[1]
    kt = k.transpose(0, 2, 1)
    ones = jnp.ones((b, sk, nkv, V7X_LANES - HEAD_DIM), v.dtype)
    vaug = jnp.concatenate([v.reshape(b, sk, nkv, HEAD_DIM), ones], axis=-1).reshape(b, sk, nkv * V7X_LANES)
    tq = _pick_tile(s, (256, 128))
    tk = _pick_tile(sk, (1280, 1024, 512, 256, 128))
    rows = grp * tq
    return pl.pallas_call(
        functools.partial(_flash_kernel, nkv, grp, base2),
        out_shape=jax.ShapeDtypeStruct((b, s, qd), BF16),
        grid=(b, s // tq, sk // tk),
        in_specs=[pl.BlockSpec((1, tq, qd), lambda i, m, j: (i, m, 0)),
                  pl.BlockSpec((1, nkv * HEAD_DIM, tk), lambda i, m, j: (i, 0, j)),
                  pl.BlockSpec((1, tk, nkv * V7X_LANES), lambda i, m, j: (i, j, 0))],
        out_specs=pl.BlockSpec((1, tq, qd), lambda i, m, j: (i, m, 0)),
        scratch_shapes=[pltpu.VMEM((nkv, rows, HEAD_DIM), BF16),
                        pltpu.VMEM((nkv, rows, V7X_LANES), F32),
                        pltpu.VMEM((nkv, rows, V7X_LANES), F32)],
        compiler_params=_cparams(("parallel", "parallel", "arbitrary")),
        name=f"flash_attn_{nkv}x{grp}",
    )(q, kt, vaug)


_NA_ROWS_PER_STEP = 8
_NA_BLOCK = _NA_ROWS_PER_STEP * GRID_W
_NA_BAND = NA_KH * GRID_W


def _na_kernel(rows, q_ref, kp_ref, kc_ref, kn_ref, vp_ref, vc_ref, vn_ref, kx_ref, vx_ref, bias_ref,
               o_ref, kcat_s, vcat_s):
    g = pl.program_id(1)
    nb = _NA_BLOCK
    for i, (kr, vr) in enumerate(((kp_ref, vp_ref), (kc_ref, vc_ref), (kn_ref, vn_ref))):
        kcat_s[i * nb:(i + 1) * nb, :] = kr[0]
        vcat_s[i * nb:(i + 1) * nb, :] = vr[0]
    n_ctx = kx_ref.shape[1]
    lanes = V7X_LANES
    head0 = lax.broadcasted_iota(I32, (GRID_W, lanes), 1) < HEAD_DIM
    ones_loc = jnp.ones((_NA_BAND, lanes), BF16)
    ones_ctx = jnp.ones((n_ctx, lanes), BF16)

    def body(j, carry):
        r = g * _NA_ROWS_PER_STEP + j
        r0 = jnp.clip(r - NA_KH // 2, 0, rows - NA_KH)
        off = pl.multiple_of((r0 - g * _NA_ROWS_PER_STEP + _NA_ROWS_PER_STEP) * GRID_W, GRID_W)
        case = r - r0
        qrows = pl.ds(pl.multiple_of(j * GRID_W, GRID_W), GRID_W)
        for pr in range(NA_HEADS // 2):
            ls = slice(pr * lanes, (pr + 1) * lanes)
            qp = q_ref[0, qrows, ls]
            zero = jnp.zeros_like(qp)
            q2 = jnp.concatenate([jnp.where(head0, qp, zero), jnp.where(head0, zero, qp)], axis=0)
            kband = kcat_s[pl.ds(off, _NA_BAND), ls]
            vband = vcat_s[pl.ds(off, _NA_BAND), ls]
            s_loc = _nt_dot(q2, kband) + bias_ref[case * (NA_HEADS // 2) + pr]
            s_ctx = _nt_dot(q2, kx_ref[0, :, ls])
            m = jnp.maximum(jnp.max(s_loc, axis=-1, keepdims=True), jnp.max(s_ctx, axis=-1, keepdims=True))
            p_loc = jnp.exp(s_loc - m).astype(BF16)
            p_ctx = jnp.exp(s_ctx - m).astype(BF16)
            pv = _dot(p_loc, vband) + _dot(p_ctx, vx_ref[0, :, ls])
            l = _dot(p_loc, ones_loc) + _dot(p_ctx, ones_ctx)
            o2 = pv / l
            o_ref[0, qrows, ls] = jnp.where(head0, o2[:GRID_W], o2[GRID_W:]).astype(o_ref.dtype)
        return carry

    lax.fori_loop(0, _NA_ROWS_PER_STEP, body, 0)


def _na_bias_table(rel_bias):
    nh, nr, nc = rel_bias.shape
    col = np.arange(GRID_W)
    c0 = np.clip(col - NA_KW // 2, 0, GRID_W - NA_KW)
    valid = (col[None, :] >= c0[:, None]) & (col[None, :] < c0[:, None] + NA_KW)
    w2 = 2 * GRID_W
    lo = GRID_W - NA_KW
    padded = jnp.pad(rel_bias.astype(F32), ((0, 0), (0, 0), (lo, w2 - lo - nc)))
    flat = jnp.broadcast_to(padded[:, :, None, :], (nh, nr, GRID_W, w2)).reshape(nh, nr, GRID_W * w2)
    skew = flat[:, :, :GRID_W * (w2 - 1)].reshape(nh, nr, GRID_W, w2 - 1)
    toe = skew[:, :, :, GRID_W - 1:]
    tab = jnp.stack([toe[:, NA_KH - 1 - case:2 * NA_KH - 1 - case] for case in range(NA_KH)])
    tab = jnp.where(valid[None, None, None], tab, NEG_BIG)
    tab = tab.transpose(0, 1, 3, 2, 4)
    return tab.reshape(NA_KH * NA_HEADS // 2, 2 * GRID_W, NA_KH * GRID_W)


def _na(q, k, v, kx, vx, bias_tab):
    b, s, d = q.shape
    l = kx.shape[1]
    rows = s // GRID_W
    assert rows % _NA_ROWS_PER_STEP == 0 and rows >= NA_KH
    ng = rows // _NA_ROWS_PER_STEP
    nb = _NA_BLOCK
    cur = pl.BlockSpec((1, nb, d), lambda i, g: (i, g, 0))
    prv = pl.BlockSpec((1, nb, d), lambda i, g: (i, jnp.maximum(g - 1, 0), 0))
    nxt = pl.BlockSpec((1, nb, d), lambda i, g: (i, jnp.minimum(g + 1, ng - 1), 0))
    ctx = pl.BlockSpec((1, l, d), lambda i, g: (i, 0, 0))
    return pl.pallas_call(
        functools.partial(_na_kernel, rows),
        out_shape=jax.ShapeDtypeStruct((b, s, d), BF16),
        grid=(b, ng),
        in_specs=[cur, prv, cur, nxt, prv, cur, nxt, ctx, ctx,
                  pl.BlockSpec(bias_tab.shape, lambda i, g: (0, 0, 0))],
        out_specs=cur,
        scratch_shapes=[pltpu.VMEM((3 * nb, d), BF16), pltpu.VMEM((3 * nb, d), BF16)],
        compiler_params=_cparams(("parallel", "parallel")),
        name="neighbourhood_attn",
    )(q, k, k, k, v, v, v, kx, vx, bias_tab)


def _softplus(x):
    return jnp.maximum(x, 0.0) + jnp.log1p(jnp.exp(-jnp.abs(x)))


def _ssd_kernel(d, final, xc_ref, xp_ref, xn_ref, dt_ref, dtt_ref, cw_ref, cb_ref, dtb_ref, dtbt_ref,
                a_ref, at_ref, h0_ref, *rest):
    if final:
        y0_ref, z_ref, dsk_ref, ng_ref, y_o, h_o, h_s = rest
    else:
        y_o, h_o, h_s = rest
    t = pl.program_id(1)
    nc = pl.num_programs(1)
    reverse = d == 1
    c = nc - 1 - t if reverse else t
    q = xc_ref.shape[1]
    hp = SSM_HEAD_DIM
    ns = SSM_STATE

    @pl.when(t == 0)
    def _():
        h_s[...] = h0_ref[0]

    pm = (c > 0).astype(F32)
    nm = (c < nc - 1).astype(F32)
    xcat = jnp.concatenate([xp_ref[0] * pm, xc_ref[0], xn_ref[0] * nm], axis=0)
    n_cat = q + 16
    acc = jnp.zeros((q, XBC_DIM), F32)
    for k in range(SSM_CONV):
        sh = (SSM_CONV // 2 - k) % n_cat
        xs_k = xcat if sh == 0 else pltpu.roll(xcat, sh, 0)
        acc = acc + xs_k[8:8 + q] * cw_ref[k:k + 1, :]
    u = _silu(acc + cb_ref[...])
    xs = u[:, :SSM_INNER]
    bm = u[:, SSM_INNER:SSM_INNER + SSM_GROUPS * ns]
    cm = u[:, SSM_INNER + SSM_GROUPS * ns:]

    dt = _softplus(dt_ref[0] + dtb_ref[...])
    dtt = _softplus(dtt_ref[0] + dtbt_ref[...])
    ii = lax.broadcasted_iota(I32, (q, q), 0)
    jj = lax.broadcasted_iota(I32, (q, q), 1)
    keep = (jj >= ii) if reverse else (jj <= ii)
    tri = keep.astype(F32)
    acs = _dot(tri, dt * a_ref[...], HIGHEST)
    acst = _nt_dot(dtt * at_ref[...], tri, HIGHEST)
    last = 0 if reverse else q - 1

    xdt_all = jnp.concatenate(
        [xs[:, h * hp:(h + 1) * hp] * dt[:, d * SSM_HEADS + h:d * SSM_HEADS + h + 1]
         for h in range(SSM_HEADS)], axis=1)
    xdt_t = [xdt_all[:, p * V7X_LANES:(p + 1) * V7X_LANES].T for p in range(SSM_INNER // V7X_LANES)]
    xdt_b = xdt_all.astype(BF16)
    gmats = []
    for gi in range(SSM_GROUPS):
        gmats.append(_nt_dot(cm[:, gi * ns:(gi + 1) * ns].astype(BF16),
                             bm[:, gi * ns:(gi + 1) * ns].astype(BF16)))
    ys = []
    for h in range(SSM_HEADS):
        col = d * SSM_HEADS + h
        gi = h // (SSM_HEADS // SSM_GROUPS)
        a_col = acs[:, col:col + 1]
        a_row = acst[col:col + 1, :]
        tot = a_col[last:last + 1, :]
        lmat = jnp.exp(jnp.where(keep, a_col - a_row, NEG_BIG))
        y_diag = _dot((gmats[gi] * lmat).astype(BF16), xdt_b[:, h * hp:(h + 1) * hp])
        hst = h_s[h]
        cg = cm[:, gi * ns:(gi + 1) * ns]
        bg = bm[:, gi * ns:(gi + 1) * ns]
        y_off = _nt_dot((cg * jnp.exp(a_col)).astype(BF16), hst.astype(BF16))
        ys.append(y_diag + y_off)
        bdec = (bg * jnp.exp(tot - a_col)).astype(BF16)
        xt = xdt_t[h // 2][(h % 2) * hp:(h % 2 + 1) * hp, :]
        h_s[h] = jnp.exp(tot) * hst + _dot(xt.astype(BF16), bdec)
    y = jnp.concatenate(ys, axis=1)

    if final:
        yt = y0_ref[0] + y + dsk_ref[...] * xs
        y_o[0] = _rms(yt * _silu(z_ref[0]), ng_ref[...]).astype(y_o.dtype)
    else:
        y_o[0] = y

    @pl.when(t == nc - 1)
    def _():
        h_o[0] = h_s[...]


def _ssd_pass(d, xbc, dt, dtt, lw, h0, fin=None):
    b, l, _ = xbc.shape
    q = SSD_CHUNK
    nc = l // q
    final = fin is not None
    cidx = (lambda t: nc - 1 - t) if d == 1 else (lambda t: t)
    r8 = q // 8
    nb8 = l // 8
    const = lambda shape: pl.BlockSpec(shape, lambda i, t: (0,) * len(shape))
    tok = lambda w: pl.BlockSpec((1, q, w), lambda i, t: (i, cidx(t), 0))
    hspec = pl.BlockSpec((1, SSM_HEADS, SSM_HEAD_DIM, SSM_STATE), lambda i, t: (i, 0, 0, 0))
    in_specs = [
        tok(XBC_DIM),
        pl.BlockSpec((1, 8, XBC_DIM), lambda i, t: (i, jnp.maximum(cidx(t) * r8 - 1, 0), 0)),
        pl.BlockSpec((1, 8, XBC_DIM), lambda i, t: (i, jnp.minimum((cidx(t) + 1) * r8, nb8 - 1), 0)),
        tok(V7X_LANES),
        pl.BlockSpec((1, 16, q), lambda i, t: (i, 0, cidx(t))),
        const((8, XBC_DIM)), const((1, XBC_DIM)), const((1, V7X_LANES)), const((16, 1)),
        const((1, V7X_LANES)), const((16, 1)), hspec]
    args = [xbc, xbc, xbc, dt, dtt, lw["conv_w"], lw["conv_b"], lw["dt_b"], lw["dt_bt"],
            lw["a_row"], lw["a_col"], h0]
    if final:
        in_specs += [tok(SSM_INNER), tok(SSM_INNER), const((1, SSM_INNER)), const((1, SSM_INNER))]
        args += [fin[0], fin[1], lw["d_skip"], lw["ssm_norm"]]
    return pl.pallas_call(
        functools.partial(_ssd_kernel, d, final),
        out_shape=[jax.ShapeDtypeStruct((b, l, SSM_INNER), BF16 if final else F32),
                   jax.ShapeDtypeStruct(h0.shape, F32)],
        grid=(b, nc),
        in_specs=in_specs,
        out_specs=[tok(SSM_INNER), hspec],
        scratch_shapes=[pltpu.VMEM((SSM_HEADS, SSM_HEAD_DIM, SSM_STATE), F32)],
        compiler_params=_cparams(("parallel", "arbitrary")),
        name=f"ssd_scan_d{d}{'_final' if final else ''}",
    )(*args)


def _merge_kernel(ya_ref, yb_ref, yc_ref, gt_ref, hx_ref, wa_ref, wb_ref, wc_ref, wo_ref, gm_ref,
                  npost_ref, npre_ref, shf_ref, scf_ref, wr_ref, hx_o, vx_o, aff_o):
    d = hx_ref.shape[2]
    gt = gt_ref[0]
    m = (jax.nn.sigmoid(gt[:, :d]) * _dot(ya_ref[0], wa_ref[...])
         + jax.nn.sigmoid(gt[:, d:2 * d]) * _dot(yb_ref[0], wb_ref[...])
         + jax.nn.sigmoid(gt[:, 2 * d:]) * _dot(yc_ref[0], wc_ref[...]))
    out = _dot(m.astype(BF16), wo_ref[...])
    hx = hx_ref[0] + gm_ref[0] * _rms(out, npost_ref[...])
    hx_o[0] = hx
    v = _rms(hx, npre_ref[...]) * (1.0 + scf_ref[0]) + shf_ref[0]
    vx_o[0] = v.astype(BF16)
    logits = _nt_dot(wr_ref[...], v, HIGHEST)
    e = jnp.exp(logits - jnp.max(logits, axis=0, keepdims=True))
    aff_o[0] = e / jnp.sum(e, axis=0, keepdims=True)


def _merge(ya, yb, yc, gates, hx, lw, gm, shf, scf):
    b, t, d = hx.shape
    tm = 256
    const = lambda shape: pl.BlockSpec(shape, lambda i, j: (0,) * len(shape))
    tok = lambda w: pl.BlockSpec((1, tm, w), lambda i, j: (i, j, 0))
    vec = pl.BlockSpec((1, 1, d), lambda i, j: (i, 0, 0))
    return pl.pallas_call(
        _merge_kernel,
        out_shape=[jax.ShapeDtypeStruct((b, t, d), F32), jax.ShapeDtypeStruct((b, t, d), BF16),
                   jax.ShapeDtypeStruct((b, N_EXPERTS, t), F32)],
        grid=(b, t // tm),
        in_specs=[tok(A_Q_DIM), tok(NA_DIM), tok(SSM_INNER), tok(N_BRANCH * d), tok(d),
                  const((A_Q_DIM, d)), const((NA_DIM, d)), const((SSM_INNER, d)), const((d, d)),
                  vec, const((1, d)), const((1, d)), vec, vec, const((N_EXPERTS, d))],
        out_specs=[tok(d), tok(d), pl.BlockSpec((1, N_EXPERTS, tm), lambda i, j: (i, 0, j))],
        compiler_params=_cparams(("parallel", "parallel")),
        name="merge_residual_router",
    )(ya, yb, yc, gates, hx, lw["w_a"], lw["w_b"], lw["w_c"], lw["w_out"], gm, lw["n_post"],
      lw["n_ffn_pre"], shf, scf, lw["w_rt"])


def _topk_kernel(cap, aff_ref, pos_o, gate_o, cs_o, eq_s, gt_s, ceq_s, csel_s):
    x = aff_ref[0]
    ne, t = x.shape
    bits = pltpu.bitcast(x, I32)
    theta = jnp.zeros((ne, 1), I32)
    for bit in range(30, -1, -1):
        cand = theta | (1 << bit)
        cnt = jnp.sum((bits >= cand).astype(I32), axis=-1, keepdims=True)
        theta = jnp.where(cnt >= cap, cand, theta)
    gt = bits > theta
    eq = bits == theta
    need = cap - jnp.sum(gt.astype(I32), axis=-1, keepdims=True)
    eq_s[...] = eq.astype(F32)
    gt_s[...] = gt.astype(F32)
    tb = ROUTE_TILE
    ii = lax.broadcasted_iota(I32, (tb, tb), 0)
    jj = lax.broadcasted_iota(I32, (tb, tb), 1)
    upper = (ii <= jj).astype(BF16)
    lane = lax.broadcasted_iota(I32, (ne, V7X_LANES), 1)
    needf = need.astype(F32)

    ceq_s[...] = jnp.zeros(ceq_s.shape, F32)
    csel_s[...] = jnp.zeros(csel_s.shape, F32)
    cs_o[0] = jnp.zeros((ne, V7X_LANES), I32)

    def body(k, carry):
        c_eq, c_sel = ceq_s[...], csel_s[...]
        sl = pl.ds(pl.multiple_of(k * tb, tb), tb)
        e_k = eq_s[:, sl]
        g_k = gt_s[:, sl]
        inc_eq = _dot(e_k.astype(BF16), upper)
        rank_eq = c_eq + inc_eq - e_k
        sel = g_k + e_k * (rank_eq < needf).astype(F32)
        inc_sel = _dot(sel.astype(BF16), upper)
        slot = c_sel + inc_sel - sel
        pos_o[0, :, sl] = jnp.where(sel > 0.5, slot, -1.0).astype(I32)
        gate_o[0, :, sl] = jnp.where(sel > 0.5, aff_ref[0, :, sl], 0.0)
        cs_o[0] = jnp.where(lane == k, c_sel.astype(I32), cs_o[0])
        ceq_s[...] = c_eq + inc_eq[:, tb - 1:tb]
        csel_s[...] = c_sel + inc_sel[:, tb - 1:tb]
        return carry

    nk = t // tb
    lax.fori_loop(0, nk, body, 0)
    cs_o[0] = jnp.where(lane == nk, csel_s[...].astype(I32), cs_o[0])


def _topk(aff_t, cap):
    b, ne, t = aff_t.shape
    assert t % ROUTE_TILE == 0 and t // ROUTE_TILE < V7X_LANES
    full = pl.BlockSpec((1, ne, t), lambda i: (i, 0, 0))
    return pl.pallas_call(
        functools.partial(_topk_kernel, cap),
        out_shape=[jax.ShapeDtypeStruct((b, ne, t), I32), jax.ShapeDtypeStruct((b, ne, t), F32),
                   jax.ShapeDtypeStruct((b, ne, V7X_LANES), I32)],
        grid=(b,),
        in_specs=[full],
        out_specs=[full, full, pl.BlockSpec((1, ne, V7X_LANES), lambda i: (i, 0, 0))],
        scratch_shapes=[pltpu.VMEM((ne, t), F32), pltpu.VMEM((ne, t), F32),
                        pltpu.VMEM((ne, 1), F32), pltpu.VMEM((ne, 1), F32)],
        compiler_params=_cparams(("parallel",)),
        name="expert_topk",
    )(aff_t)


def _cs_at(cs_ref, b, e, k):
    return cs_ref[(b * N_EXPERTS + e) * V7X_LANES + k]


def _gather_kernel(sb, nblk, cs_ref, pos_ref, vx_ref, xs_o, cur_s, nxt_s):
    b, e, k = pl.program_id(0), pl.program_id(1), pl.program_id(2)
    blk = jnp.minimum(_cs_at(cs_ref, b, e, k) // sb, nblk - 1)
    blk_prev = jnp.minimum(_cs_at(cs_ref, b, e, jnp.maximum(k - 1, 0)) // sb, nblk - 1)

    @pl.when(k == 0)
    def _():
        cur_s[...] = jnp.zeros(cur_s.shape, F32)
        nxt_s[...] = jnp.zeros(nxt_s.shape, F32)

    @pl.when(jnp.logical_and(k > 0, blk != blk_prev))
    def _():
        cur_s[...] = nxt_s[...]
        nxt_s[...] = jnp.zeros(nxt_s.shape, F32)

    tt = vx_ref.shape[1]
    local = pos_ref[0, 0] - blk * sb
    onehot = (lax.broadcasted_iota(I32, (2 * sb, tt), 0) == local).astype(BF16)
    rows = _dot(onehot, vx_ref[0])
    cur_s[...] += rows[:sb]
    nxt_s[...] += rows[sb:]
    xs_o[0, 0] = cur_s[...].astype(xs_o.dtype)


def _gather(cs_flat, pos_t, vx, cap):
    b, t, d = vx.shape
    ne = N_EXPERTS
    tt = ROUTE_TILE
    sb = min(SLOT_BLOCK, cap)
    nblk = cap // sb
    blk_of = lambda i, e, k, cs: jnp.minimum(cs[(i * ne + e) * V7X_LANES + k] // sb, nblk - 1)
    grid_spec = pltpu.PrefetchScalarGridSpec(
        num_scalar_prefetch=1,
        grid=(b, ne, t // tt),
        in_specs=[pl.BlockSpec((1, 1, 1, tt), lambda i, e, k, cs: (i, e, 0, k)),
                  pl.BlockSpec((1, tt, d), lambda i, e, k, cs: (i, k, 0))],
        out_specs=pl.BlockSpec((1, 1, sb, d), lambda i, e, k, cs: (i, e, blk_of(i, e, k, cs), 0)),
        scratch_shapes=[pltpu.VMEM((sb, d), F32), pltpu.VMEM((sb, d), F32)])
    return pl.pallas_call(
        functools.partial(_gather_kernel, sb, nblk),
        out_shape=jax.ShapeDtypeStruct((b, ne, cap, d), BF16),
        grid_spec=grid_spec,
        compiler_params=_cparams(("parallel", "parallel", "arbitrary")),
        name="expert_gather",
    )(cs_flat, pos_t.reshape(b, ne, 1, t), vx)


def _ffn_kernel(x_ref, wg_ref, wu_ref, wd_ref, y_o):
    x = x_ref[0, 0]
    f = wg_ref.shape[2]
    tf = 512
    acc = jnp.zeros((x.shape[0], wd_ref.shape[2]), F32)
    for i in range(f // tf):
        sl = slice(i * tf, (i + 1) * tf)
        hid = _silu(_dot(x, wg_ref[0, :, sl])) * _dot(x, wu_ref[0, :, sl])
        acc = acc + _dot(hid.astype(BF16), wd_ref[0, sl, :])
    y_o[0, 0] = acc.astype(y_o.dtype)


def _expert_ffn(xs, wg, wu, wd):
    b, ne, cap, d = xs.shape
    f = wg.shape[2]
    tc = min(512, cap)
    tok = pl.BlockSpec((1, 1, tc, d), lambda e, i, c: (i, e, c, 0))
    return pl.pallas_call(
        _ffn_kernel,
        out_shape=jax.ShapeDtypeStruct(xs.shape, BF16),
        grid=(ne, b, cap // tc),
        in_specs=[tok,
                  pl.BlockSpec((1, d, f), lambda e, i, c: (e, 0, 0)),
                  pl.BlockSpec((1, d, f), lambda e, i, c: (e, 0, 0)),
                  pl.BlockSpec((1, f, d), lambda e, i, c: (e, 0, 0))],
        out_specs=tok,
        compiler_params=_cparams(("parallel", "parallel", "parallel")),
        name="expert_ffn",
    )(xs, wg, wu, wd)


def _combine_kernel(sb, nblk, cs_ref, y0_ref, y1_ref, pos_ref, gate_ref, hx_ref, gf_ref, npost_ref,
                    hx_o, acc_s):
    b, k, e = pl.program_id(0), pl.program_id(1), pl.program_id(2)
    blk = jnp.minimum(_cs_at(cs_ref, b, e, k) // sb, nblk - 1)

    @pl.when(e == 0)
    def _():
        acc_s[...] = jnp.zeros(acc_s.shape, F32)

    tt = hx_ref.shape[1]
    lane = lax.broadcasted_iota(I32, (tt, N_EXPERTS), 1)
    mine = lane == e
    slot = jnp.sum(jnp.where(mine, pos_ref[0], 0), axis=-1, keepdims=True)
    gate = jnp.sum(jnp.where(mine, gate_ref[0], 0.0), axis=-1, keepdims=True)
    local = slot - blk * sb
    onehot = (lax.broadcasted_iota(I32, (tt, 2 * sb), 1) == local).astype(BF16)
    ycat = jnp.concatenate([y0_ref[0, 0], y1_ref[0, 0]], axis=0)
    acc_s[...] += gate * _dot(onehot, ycat)

    @pl.when(e == pl.num_programs(2) - 1)
    def _():
        hx_o[0] = hx_ref[0] + gf_ref[0] * _rms(acc_s[...], npost_ref[...])


def _combine(cs_flat, ys, pos_r, gate_r, hx, gf, n_post):
    b, t, d = hx.shape
    ne, cap = ys.shape[1], ys.shape[2]
    tt = ROUTE_TILE
    sb = min(SLOT_BLOCK, cap)
    nblk = cap // sb
    blk_of = lambda i, k, e, cs: jnp.minimum(cs[(i * ne + e) * V7X_LANES + k] // sb, nblk - 1)
    tok = lambda w: pl.BlockSpec((1, tt, w), lambda i, k, e, cs: (i, k, 0))
    grid_spec = pltpu.PrefetchScalarGridSpec(
        num_scalar_prefetch=1,
        grid=(b, t // tt, ne),
        in_specs=[pl.BlockSpec((1, 1, sb, d), lambda i, k, e, cs: (i, e, blk_of(i, k, e, cs), 0)),
                  pl.BlockSpec((1, 1, sb, d),
                               lambda i, k, e, cs: (i, e, jnp.minimum(blk_of(i, k, e, cs) + 1, nblk - 1), 0)),
                  tok(ne), tok(ne), tok(d),
                  pl.BlockSpec((1, 1, d), lambda i, k, e, cs: (i, 0, 0)),
                  pl.BlockSpec((1, d), lambda i, k, e, cs: (0, 0))],
        out_specs=tok(d),
        scratch_shapes=[pltpu.VMEM((tt, d), F32)])
    return pl.pallas_call(
        functools.partial(_combine_kernel, sb, nblk),
        out_shape=jax.ShapeDtypeStruct((b, t, d), F32),
        grid_spec=grid_spec,
        compiler_params=_cparams(("parallel", "parallel", "arbitrary")),
        name="expert_combine",
    )(cs_flat, ys, ys, pos_r, gate_r, hx, gf, n_post)


def _moe(vx, aff_t, hx, gf, lw):
    b, t, d = hx.shape
    cap = EC_CAPACITY * t // N_EXPERTS
    pos_t, gate_t, cs = _topk(aff_t, cap)
    cs_flat = cs.reshape(-1)
    xs = _gather(cs_flat, pos_t, vx, cap)
    ys = _expert_ffn(xs, lw["w_gate"], lw["w_up"], lw["w_down"])
    return _combine(cs_flat, ys, pos_t.transpose(0, 2, 1), gate_t.transpose(0, 2, 1), hx, gf,
                    lw["n_ffn_post"])


def _rope_tables(s):
    t = jnp.arange(s, dtype=I32)
    rows = (t // GRID_W).astype(F32)
    cols = (t % GRID_W).astype(F32)
    n_freq = HEAD_DIM // 4
    inv = ROPE_BASE ** (-jnp.arange(n_freq, dtype=F32) / n_freq)
    ar = rows[:, None] * inv[None, :]
    ac = cols[:, None] * inv[None, :]
    ang = jnp.concatenate([ar, ar, ac, ac] * (V7X_LANES // HEAD_DIM), axis=-1)
    return jnp.cos(ang), jnp.sin(ang)


def _layer_weights(i, p):
    d = p["w_in"].shape[1]
    w_in = p["w_in"][i]
    dt_pad = jnp.zeros((d, V7X_LANES - N_DIRS * SSM_HEADS), F32)
    hm = np.kron(np.eye(A_HEADS, dtype=np.float32), np.full((HEAD_DIM, HEAD_DIM), 1.0 / HEAD_DIM, np.float32))
    lane_pad = lambda v: jnp.pad(v.reshape(1, -1), ((0, 0), (0, V7X_LANES - v.size)))
    a = -jnp.exp(p["ssm_a_log"][i].astype(F32)).reshape(-1)
    return {
        "w_main": jnp.concatenate([w_in[:, :_DT_COL], w_in[:, _DT_COL + N_DIRS * SSM_HEADS:]], axis=1).astype(BF16),
        "w_dt": jnp.concatenate([w_in[:, _DT_COL:_DT_COL + N_DIRS * SSM_HEADS], dt_pad], axis=1).astype(BF16),
        "qn": jnp.tile(p["q_norm"][i], A_HEADS).reshape(1, -1),
        "kn": jnp.tile(p["k_norm"][i], A_KV_HEADS).reshape(1, -1),
        "hm": jnp.asarray(hm, BF16),
        "n_pre": p["norm_mix_pre"][i].reshape(1, d),
        "n_post": p["norm_mix_post"][i].reshape(1, d),
        "n_ffn_pre": p["norm_ffn_pre"][i].reshape(1, d),
        "n_ffn_post": p["norm_ffn_post"][i].reshape(1, d),
        "bias_tab": _na_bias_table(p["na_rel_bias"][i]),
        "conv_w": jnp.pad(p["conv_w"][i], ((0, 8 - SSM_CONV), (0, 0))),
        "conv_b": p["conv_b"][i].reshape(1, -1),
        "dt_b": lane_pad(p["ssm_dt_bias"][i]),
        "dt_bt": p["ssm_dt_bias"][i].reshape(-1, 1),
        "a_row": lane_pad(a),
        "a_col": a.reshape(-1, 1),
        "d_skip": jnp.repeat(p["ssm_d"][i, 0] + p["ssm_d"][i, 1], SSM_HEAD_DIM).reshape(1, -1),
        "ssm_norm": p["ssm_norm"][i].reshape(1, -1),
        "w_a": p["w_branch_a"][i].astype(BF16),
        "w_b": p["w_branch_b"][i].astype(BF16),
        "w_c": p["w_branch_c"][i].astype(BF16),
        "w_out": p["w_out"][i].astype(BF16),
        "w_rt": p["w_router"][i].T,
        "w_gate": p["w_exp_gate"][i].astype(BF16),
        "w_up": p["w_exp_up"][i].astype(BF16),
        "w_down": p["w_exp_down"][i].astype(BF16),
    }


def _ssd_branch(px, pc, lw, with_ctx_out):
    zx, xbcx, dtx = px
    zc, xbcc, dtc = pc
    b = zx.shape[0]
    nd = N_DIRS * SSM_HEADS
    dttx = dtx[:, :, :nd].transpose(0, 2, 1)
    dttc = dtc[:, :, :nd].transpose(0, 2, 1)
    h0 = jnp.zeros((b, SSM_HEADS, SSM_HEAD_DIM, SSM_STATE), F32)
    y0c, hc0 = _ssd_pass(0, xbcc, dtc, dttc, lw, h0)
    yc, hc1 = _ssd_pass(1, xbcc, dtc, dttc, lw, h0, fin=(y0c, zc))
    y0x, _ = _ssd_pass(0, xbcx, dtx, dttx, lw, hc0)
    yx, _ = _ssd_pass(1, xbcx, dtx, dttx, lw, hc1, fin=(y0x, zx))
    return yx, (yc if with_ctx_out else None)


def _layer(i, hx, hc, mod_x, mod_c, p, rope_tabs, last):
    lw = _layer_weights(i, p)
    d = hx.shape[2]
    sh_m, sc_m, g_m, sh_f, sc_f, g_f = [mod_x[:, None, j * d:(j + 1) * d] for j in range(N_ADA)]
    csh_m, csc_m, cg_m, csh_f, csc_f, cg_f = [mod_c[:, None, j * d:(j + 1) * d] for j in range(N_ADA)]

    qa_c, ka_c, va_c, qb_c, kb_c, vb_c, z_c, xbc_c, dt_c, gt_c = _proj(hc, lw["n_pre"], csh_m, csc_m, lw, None)
    qa, ka, va, qb, kb, vb, z, xbc, dt, gt = _proj(hx, lw["n_pre"], sh_m, sc_m, lw, rope_tabs)

    ya = _flash(qa, jnp.concatenate([ka, ka_c], axis=1), jnp.concatenate([va, va_c], axis=1),
                A_KV_HEADS, A_HEADS // A_KV_HEADS, True)
    yb = _na(qb, kb, vb, kb_c, vb_c, lw["bias_tab"])
    yc, yc_c = _ssd_branch((z, xbc, dt), (z_c, xbc_c, dt_c), lw, not last)
    hx, vx, aff = _merge(ya, yb, yc, gt, hx, lw, g_m, sh_f, sc_f)
    hx = _moe(vx, aff, hx, g_f, lw)
    if last:
        return hx, hc
    ya_c = _flash(qa_c, ka_c, va_c, A_KV_HEADS, A_HEADS // A_KV_HEADS, True)
    yb_c = _flash(qb_c, kb_c, vb_c, NA_HEADS, 1, False)
    hc, vc, aff_c = _merge(ya_c, yb_c, yc_c, gt_c, hc, lw, cg_m, csh_f, csc_f)
    hc = _moe(vc, aff_c, hc, cg_f, lw)
    return hx, hc


def kernel(x, c, ctx, c_ctx, w_ada, b_ada, norm_mix_pre, norm_mix_post, norm_ffn_pre, norm_ffn_post, w_in, q_norm, k_norm, na_rel_bias, conv_w, conv_b, ssm_a_log, ssm_dt_bias, ssm_d, ssm_norm, w_branch_a, w_branch_b, w_branch_c, w_out, w_router, w_exp_gate, w_exp_up, w_exp_down):
    p = dict(norm_mix_pre=norm_mix_pre, norm_mix_post=norm_mix_post, norm_ffn_pre=norm_ffn_pre,
             norm_ffn_post=norm_ffn_post, w_in=w_in, q_norm=q_norm, k_norm=k_norm, na_rel_bias=na_rel_bias,
             conv_w=conv_w, conv_b=conv_b, ssm_a_log=ssm_a_log, ssm_dt_bias=ssm_dt_bias, ssm_d=ssm_d,
             ssm_norm=ssm_norm, w_branch_a=w_branch_a, w_branch_b=w_branch_b, w_branch_c=w_branch_c,
             w_out=w_out, w_router=w_router, w_exp_gate=w_exp_gate, w_exp_up=w_exp_up, w_exp_down=w_exp_down)
    b, s, d = x.shape
    depth = w_ada.shape[0]
    assert b + 1 <= 8
    cc = jnp.concatenate([c, c_ctx[None, :], jnp.zeros((8 - b - 1, d), F32)], axis=0)
    mod = _ada_mod(cc, w_ada, b_ada)
    rope_tabs = _rope_tables(s)
    hx, hc = x, ctx
    for i in range(depth):
        mod_x = mod[i, :b]
        mod_c = jnp.broadcast_to(mod[i, b:b + 1], (b, mod.shape[2]))
        hx, hc = _layer(i, hx, hc, mod_x, mod_c, p, rope_tabs, i == depth - 1)
    return hx
```

```python
import functools

import jax
import jax.numpy as jnp
import numpy as np
from jax import lax
from jax.experimental import pallas as pl
from jax.experimental.pallas import tpu as pltpu

F32, BF16, I32 = jnp.float32, jnp.bfloat16, jnp.int32
HIGHEST = lax.Precision.HIGHEST

GRID_W = 64
HEAD_DIM = 64
ROPE_BASE = 10000.0
NORM_EPS = 1e-6
N_ADA = 6
A_HEADS, A_KV_HEADS = 8, 2
NA_HEADS, NA_KH, NA_KW = 8, 8, 16
SSM_HEADS, SSM_HEAD_DIM, SSM_INNER = 8, 64, 512
SSM_GROUPS, SSM_STATE, SSM_CONV, N_DIRS = 2, 64, 5, 2
XBC_DIM = SSM_INNER + 2 * SSM_GROUPS * SSM_STATE
A_Q_DIM, A_KV_DIM, NA_DIM = A_HEADS * HEAD_DIM, A_KV_HEADS * HEAD_DIM, NA_HEADS * HEAD_DIM
N_EXPERTS, EC_CAPACITY = 16, 2
N_BRANCH = 3
ATT_SCALE = HEAD_DIM ** -0.5
LOG2_E = 1.4426950408889634

V7X_LANES = 128
V7X_VMEM_LIMIT_BYTES = 56 * 1024 * 1024
NEG_BIG = -1e30

SSD_CHUNK = 128
ROUTE_TILE = 256
SLOT_BLOCK = 256


def _cparams(sem):
    return pltpu.CompilerParams(dimension_semantics=sem, vmem_limit_bytes=V7X_VMEM_LIMIT_BYTES)


def _nt_dot(a, b, precision=None):
    return lax.dot_general(a, b, (((1,), (1,)), ((), ())), preferred_element_type=F32,
                           precision=precision)


def _dot(a, b, precision=None):
    return jnp.dot(a, b, preferred_element_type=F32, precision=precision)


def _silu(x):
    return x * jax.nn.sigmoid(x)


def _rms(x, gain):
    return x * lax.rsqrt(jnp.mean(x * x, axis=-1, keepdims=True) + NORM_EPS) * gain


def _ada_kernel(c_ref, w_ref, b_ref, o_ref):
    o_ref[0] = _dot(_silu(c_ref[...]), w_ref[0], HIGHEST) + b_ref[0]


def _ada_mod(cc, w_ada, b_ada):
    nl, d, n = w_ada.shape
    tn = 512
    return pl.pallas_call(
        _ada_kernel,
        out_shape=jax.ShapeDtypeStruct((nl, 8, n), F32),
        grid=(nl, n // tn),
        in_specs=[pl.BlockSpec((8, d), lambda l, j: (0, 0)),
                  pl.BlockSpec((1, d, tn), lambda l, j: (l, 0, j)),
                  pl.BlockSpec((1, 1, tn), lambda l, j: (l, 0, j))],
        out_specs=pl.BlockSpec((1, 8, tn), lambda l, j: (l, 0, j)),
        compiler_params=_cparams(("parallel", "parallel")),
        name="ada_mod",
    )(cc, w_ada, b_ada.reshape(nl, 1, n))


_OFF_QA, _OFF_KA, _OFF_VA, _OFF_QB, _OFF_KB, _OFF_VB, _OFF_Z, _OFF_XBC, _OFF_G, _OFF_END = (
    0, 512, 640, 768, 1280, 1792, 2304, 2816, 3584, 6656)
_DT_COL = 3584


def _head_rms(x, hm, gain):
    ms = _dot((x * x).astype(BF16), hm)
    return x * lax.rsqrt(ms + NORM_EPS) * gain


def _rope(x, cos, sin):
    w = x.shape[-1]
    lane = lax.broadcasted_iota(I32, x.shape, 1)
    first = (lane & 16) == 0
    rot = jnp.where(first, -pltpu.roll(x, w - 16, 1), pltpu.roll(x, 16, 1))
    return x * cos + rot * sin


def _proj_kernel(rope, h_ref, g_ref, sh_ref, sc_ref, w_ref, wdt_ref, qn_ref, kn_ref, hm_ref, *rest):
    if rope:
        cos_ref, sin_ref = rest[:2]
        rest = rest[2:]
    qa_o, ka_o, va_o, qb_o, kb_o, vb_o, z_o, xbc_o, dt_o, gt_o = rest
    u = _rms(h_ref[0], g_ref[...]) * (1.0 + sc_ref[0]) + sh_ref[0]
    ub = u.astype(BF16)

    def mm(lo, hi):
        return _dot(ub, w_ref[:, lo:hi])

    qa = _head_rms(mm(_OFF_QA, _OFF_KA), hm_ref[...], qn_ref[...])
    ka = _head_rms(mm(_OFF_KA, _OFF_VA), hm_ref[:A_KV_DIM, :A_KV_DIM], kn_ref[...])
    if rope:
        cs, sn = cos_ref[...], sin_ref[...]
        qa = _rope(qa, jnp.concatenate([cs] * 4, axis=1), jnp.concatenate([sn] * 4, axis=1))
        ka = _rope(ka, cs, sn)
    qa_o[0] = (qa * (ATT_SCALE * LOG2_E)).astype(BF16)
    ka_o[0] = ka.astype(BF16)
    va_o[0] = mm(_OFF_VA, _OFF_QB).astype(BF16)
    qb_o[0] = (mm(_OFF_QB, _OFF_KB) * ATT_SCALE).astype(BF16)
    kb_o[0] = mm(_OFF_KB, _OFF_VB).astype(BF16)
    vb_o[0] = mm(_OFF_VB, _OFF_Z).astype(BF16)
    z_o[0] = mm(_OFF_Z, _OFF_XBC)
    xbc_o[0] = mm(_OFF_XBC, _OFF_G)
    dt_o[0] = _dot(ub, wdt_ref[...])
    for i in range((_OFF_END - _OFF_G) // 512):
        gt_o[0, :, i * 512:(i + 1) * 512] = mm(_OFF_G + i * 512, _OFF_G + (i + 1) * 512)


def _proj(h, gain, shift, scale, lw, rope_tabs):
    b, t, d = h.shape
    tm = 256
    rope = rope_tabs is not None
    const = lambda shape: pl.BlockSpec(shape, lambda i, j: (0,) * len(shape))
    tok = lambda w: pl.BlockSpec((1, tm, w), lambda i, j: (i, j, 0))
    vec = pl.BlockSpec((1, 1, d), lambda i, j: (i, 0, 0))
    in_specs = [tok(d), const((1, d)), vec, vec, const((d, _OFF_END)), const((d, V7X_LANES)),
                const((1, A_Q_DIM)), const((1, A_KV_DIM)), const((A_Q_DIM, A_Q_DIM))]
    args = [h, gain, shift, scale, lw["w_main"], lw["w_dt"], lw["qn"], lw["kn"], lw["hm"]]
    if rope:
        tab = pl.BlockSpec((tm, V7X_LANES), lambda i, j: (j, 0))
        in_specs += [tab, tab]
        args += list(rope_tabs)
    widths = [(A_Q_DIM, BF16), (A_KV_DIM, BF16), (A_KV_DIM, BF16), (NA_DIM, BF16), (NA_DIM, BF16),
              (NA_DIM, BF16), (SSM_INNER, F32), (XBC_DIM, F32), (V7X_LANES, F32),
              (_OFF_END - _OFF_G, F32)]
    return pl.pallas_call(
        functools.partial(_proj_kernel, rope),
        out_shape=[jax.ShapeDtypeStruct((b, t, w), dt) for w, dt in widths],
        grid=(b, t // tm),
        in_specs=in_specs,
        out_specs=[tok(w) for w, _ in widths],
        compiler_params=_cparams(("parallel", "parallel")),
        name="norm_mod_proj",
    )(*args)


def _flash_kernel(nkv, grp, base2, q_ref, kt_ref, v_ref, o_ref, q_s, m_s, acc_s):
    j = pl.program_id(2)
    tq = q_ref.shape[1]
    tk = kt_ref.shape[2]
    hd = HEAD_DIM
    ex = jnp.exp2 if base2 else jnp.exp

    @pl.when(j == 0)
    def _():
        qb = q_ref[0]
        for g in range(nkv):
            parts = [qb[:, (g * grp + i) * hd:(g * grp + i + 1) * hd] for i in range(grp)]
            q_s[g] = parts[0] if grp == 1 else jnp.concatenate(parts, axis=0)
        m_s[...] = jnp.full(m_s.shape, -jnp.inf, F32)
        acc_s[...] = jnp.zeros(acc_s.shape, F32)

    for g in range(nkv):
        s = _dot(q_s[g], kt_ref[0, g * hd:(g + 1) * hd, :])
        m_prev = m_s[g]
        m_new = jnp.maximum(m_prev, jnp.max(s, axis=-1, keepdims=True))
        alpha = ex(m_prev - m_new)
        m_s[g] = m_new
        p = jnp.concatenate(
            [ex(s[:, c * V7X_LANES:(c + 1) * V7X_LANES] - m_new).astype(BF16) for c in range(tk // V7X_LANES)],
            axis=1)
        acc_s[g] = alpha * acc_s[g] + _dot(p, v_ref[0, :, g * V7X_LANES:(g + 1) * V7X_LANES])

    @pl.when(j == pl.num_programs(2) - 1)
    def _():
        for g in range(nkv):
            acc = acc_s[g]
            o = acc / pltpu.roll(acc, hd, 1)
            for i in range(grp):
                c0 = (g * grp + i) * hd
                o_ref[0, :, c0:c0 + hd] = o[i * tq:(i + 1) * tq, :hd].astype(o_ref.dtype)


def _pick_tile(n, cands):
    for c in cands:
        if n % c == 0:
            return c
    raise ValueError(f"no tile for {n}")


def _flash(q, k, v, nkv, grp, base2):
    b, s, qd = q.shape
    sk = k.shape[1]
    kt = k.transpose(0, 2, 1)
    ones = jnp.ones((b, sk, nkv, V7X_LANES - HEAD_DIM), v.dtype)
    vaug = jnp.concatenate([v.reshape(b, sk, nkv, HEAD_DIM), ones], axis=-1).reshape(b, sk, nkv * V7X_LANES)
    tq = _pick_tile(s, (256, 128))
    tk = _pick_tile(sk, (3328, 1280, 1024, 512, 256, 128))
    rows = grp * tq
    return pl.pallas_call(
        functools.partial(_flash_kernel, nkv, grp, base2),
        out_shape=jax.ShapeDtypeStruct((b, s, qd), BF16),
        grid=(b, s // tq, sk // tk),
        in_specs=[pl.BlockSpec((1, tq, qd), lambda i, m, j: (i, m, 0)),
                  pl.BlockSpec((1, nkv * HEAD_DIM, tk), lambda i, m, j: (i, 0, j)),
                  pl.BlockSpec((1, tk, nkv * V7X_LANES), lambda i, m, j: (i, j, 0))],
        out_specs=pl.BlockSpec((1, tq, qd), lambda i, m, j: (i, m, 0)),
        scratch_shapes=[pltpu.VMEM((nkv, rows, HEAD_DIM), BF16),
                        pltpu.VMEM((nkv, rows, V7X_LANES), F32),
                        pltpu.VMEM((nkv, rows, V7X_LANES), F32)],
        compiler_params=_cparams(("parallel", "parallel", "arbitrary")),
        name=f"flash_attn_{nkv}x{grp}",
    )(q, kt, vaug)


_NA_ROWS_PER_STEP = 8
_NA_BLOCK = _NA_ROWS_PER_STEP * GRID_W
_NA_BAND = NA_KH * GRID_W


def _na_kernel(rows, q_ref, kp_ref, kc_ref, kn_ref, vp_ref, vc_ref, vn_ref, kx_ref, vx_ref, bias_ref,
               o_ref, kcat_s, vcat_s):
    g = pl.program_id(1)
    nb = _NA_BLOCK
    for i, (kr, vr) in enumerate(((kp_ref, vp_ref), (kc_ref, vc_ref), (kn_ref, vn_ref))):
        kcat_s[i * nb:(i + 1) * nb, :] = kr[0]
        vcat_s[i * nb:(i + 1) * nb, :] = vr[0]
    n_ctx = kx_ref.shape[1]
    lanes = V7X_LANES
    head0 = lax.broadcasted_iota(I32, (GRID_W, lanes), 1) < HEAD_DIM
    ones_loc = jnp.ones((_NA_BAND, lanes), BF16)
    ones_ctx = jnp.ones((n_ctx, lanes), BF16)

    def body(j, carry):
        r = g * _NA_ROWS_PER_STEP + j
        r0 = jnp.clip(r - NA_KH // 2, 0, rows - NA_KH)
        off = pl.multiple_of((r0 - g * _NA_ROWS_PER_STEP + _NA_ROWS_PER_STEP) * GRID_W, GRID_W)
        case = r - r0
        qrows = pl.ds(pl.multiple_of(j * GRID_W, GRID_W), GRID_W)
        for pr in range(NA_HEADS // 2):
            ls = slice(pr * lanes, (pr + 1) * lanes)
            qp = q_ref[0, qrows, ls]
            zero = jnp.zeros_like(qp)
            q2 = jnp.concatenate([jnp.where(head0, qp, zero), jnp.where(head0, zero, qp)], axis=0)
            kband = kcat_s[pl.ds(off, _NA_BAND), ls]
            vband = vcat_s[pl.ds(off, _NA_BAND), ls]
            s_loc = _nt_dot(q2, kband) + bias_ref[case * (NA_HEADS // 2) + pr]
            s_ctx = _nt_dot(q2, kx_ref[0, :, ls])
            m = jnp.maximum(jnp.max(s_loc, axis=-1, keepdims=True), jnp.max(s_ctx, axis=-1, keepdims=True))
            p_loc = jnp.exp(s_loc - m).astype(BF16)
            p_ctx = jnp.exp(s_ctx - m).astype(BF16)
            pv = _dot(p_loc, vband) + _dot(p_ctx, vx_ref[0, :, ls])
            l = _dot(p_loc, ones_loc) + _dot(p_ctx, ones_ctx)
            o2 = pv / l
            o_ref[0, qrows, ls] = jnp.where(head0, o2[:GRID_W], o2[GRID_W:]).astype(o_ref.dtype)
        return carry

    lax.fori_loop(0, _NA_ROWS_PER_STEP, body, 0)


def _na_bias_table(rel_bias):
    nh, nr, nc = rel_bias.shape
    col = np.arange(GRID_W)
    c0 = np.clip(col - NA_KW // 2, 0, GRID_W - NA_KW)
    valid = (col[None, :] >= c0[:, None]) & (col[None, :] < c0[:, None] + NA_KW)
    w2 = 2 * GRID_W
    lo = GRID_W - NA_KW
    padded = jnp.pad(rel_bias.astype(F32), ((0, 0), (0, 0), (lo, w2 - lo - nc)))
    flat = jnp.broadcast_to(padded[:, :, None, :], (nh, nr, GRID_W, w2)).reshape(nh, nr, GRID_W * w2)
    skew = flat[:, :, :GRID_W * (w2 - 1)].reshape(nh, nr, GRID_W, w2 - 1)
    toe = skew[:, :, :, GRID_W - 1:]
    tab = jnp.stack([toe[:, NA_KH - 1 - case:2 * NA_KH - 1 - case] for case in range(NA_KH)])
    tab = jnp.where(valid[None, None, None], tab, NEG_BIG)
    tab = tab.transpose(0, 1, 3, 2, 4)
    return tab.reshape(NA_KH * NA_HEADS // 2, 2 * GRID_W, NA_KH * GRID_W)


def _na(q, k, v, kx, vx, bias_tab):
    b, s, d = q.shape
    l = kx.shape[1]
    rows = s // GRID_W
    assert rows % _NA_ROWS_PER_STEP == 0 and rows >= NA_KH
    ng = rows // _NA_ROWS_PER_STEP
    nb = _NA_BLOCK
    cur = pl.BlockSpec((1, nb, d), lambda i, g: (i, g, 0))
    prv = pl.BlockSpec((1, nb, d), lambda i, g: (i, jnp.maximum(g - 1, 0), 0))
    nxt = pl.BlockSpec((1, nb, d), lambda i, g: (i, jnp.minimum(g + 1, ng - 1), 0))
    ctx = pl.BlockSpec((1, l, d), lambda i, g: (i, 0, 0))
    return pl.pallas_call(
        functools.partial(_na_kernel, rows),
        out_shape=jax.ShapeDtypeStruct((b, s, d), BF16),
        grid=(b, ng),
        in_specs=[cur, prv, cur, nxt, prv, cur, nxt, ctx, ctx,
                  pl.BlockSpec(bias_tab.shape, lambda i, g: (0, 0, 0))],
        out_specs=cur,
        scratch_shapes=[pltpu.VMEM((3 * nb, d), BF16), pltpu.VMEM((3 * nb, d), BF16)],
        compiler_params=_cparams(("parallel", "parallel")),
        name="neighbourhood_attn",
    )(q, k, k, k, v, v, v, kx, vx, bias_tab)


def _softplus(x):
    return jnp.maximum(x, 0.0) + jnp.log1p(jnp.exp(-jnp.abs(x)))


def _ssd_kernel(d, final, xc_ref, xp_ref, xn_ref, dt_ref, dtt_ref, cw_ref, cb_ref, dtb_ref, dtbt_ref,
                a_ref, at_ref, h0_ref, *rest):
    if final:
        y0_ref, z_ref, dsk_ref, ng_ref, y_o, h_o, h_s = rest
    else:
        y_o, h_o, h_s = rest
    t = pl.program_id(1)
    nc = pl.num_programs(1)
    reverse = d == 1
    c = nc - 1 - t if reverse else t
    q = xc_ref.shape[1]
    hp = SSM_HEAD_DIM
    ns = SSM_STATE

    @pl.when(t == 0)
    def _():
        h_s[...] = h0_ref[0]

    pm = (c > 0).astype(F32)
    nm = (c < nc - 1).astype(F32)
    xcat = jnp.concatenate([xp_ref[0] * pm, xc_ref[0], xn_ref[0] * nm], axis=0)
    n_cat = q + 16
    acc = jnp.zeros((q, XBC_DIM), F32)
    for k in range(SSM_CONV):
        sh = (SSM_CONV // 2 - k) % n_cat
        xs_k = xcat if sh == 0 else pltpu.roll(xcat, sh, 0)
        acc = acc + xs_k[8:8 + q] * cw_ref[k:k + 1, :]
    u = _silu(acc + cb_ref[...])
    xs = u[:, :SSM_INNER]
    bm = u[:, SSM_INNER:SSM_INNER + SSM_GROUPS * ns]
    cm = u[:, SSM_INNER + SSM_GROUPS * ns:]

    dt = _softplus(dt_ref[0] + dtb_ref[...])
    dtt = _softplus(dtt_ref[0] + dtbt_ref[...])
    ii = lax.broadcasted_iota(I32, (q, q), 0)
    jj = lax.broadcasted_iota(I32, (q, q), 1)
    keep = (jj >= ii) if reverse else (jj <= ii)
    tri = keep.astype(F32)
    acs = _dot(tri, dt * a_ref[...], HIGHEST)
    acst = _nt_dot(dtt * at_ref[...], tri, HIGHEST)
    last = 0 if reverse else q - 1

    xdt_all = jnp.concatenate(
        [xs[:, h * hp:(h + 1) * hp] * dt[:, d * SSM_HEADS + h:d * SSM_HEADS + h + 1]
         for h in range(SSM_HEADS)], axis=1)
    xdt_t = [xdt_all[:, p * V7X_LANES:(p + 1) * V7X_LANES].T for p in range(SSM_INNER // V7X_LANES)]
    xdt_b = xdt_all.astype(BF16)
    gmats = []
    for gi in range(SSM_GROUPS):
        gmats.append(_nt_dot(cm[:, gi * ns:(gi + 1) * ns].astype(BF16),
                             bm[:, gi * ns:(gi + 1) * ns].astype(BF16)))
    ys = []
    for h in range(SSM_HEADS):
        col = d * SSM_HEADS + h
        gi = h // (SSM_HEADS // SSM_GROUPS)
        a_col = acs[:, col:col + 1]
        a_row = acst[col:col + 1, :]
        tot = a_col[last:last + 1, :]
        lmat = jnp.exp(jnp.where(keep, a_col - a_row, NEG_BIG))
        y_diag = _dot((gmats[gi] * lmat).astype(BF16), xdt_b[:, h * hp:(h + 1) * hp])
        hst = h_s[h]
        cg = cm[:, gi * ns:(gi + 1) * ns]
        bg = bm[:, gi * ns:(gi + 1) * ns]
        y_off = _nt_dot((cg * jnp.exp(a_col)).astype(BF16), hst.astype(BF16))
        ys.append(y_diag + y_off)
        bdec = (bg * jnp.exp(tot - a_col)).astype(BF16)
        xt = xdt_t[h // 2][(h % 2) * hp:(h % 2 + 1) * hp, :]
        h_s[h] = jnp.exp(tot) * hst + _dot(xt.astype(BF16), bdec)
    y = jnp.concatenate(ys, axis=1)

    if final:
        yt = y0_ref[0] + y + dsk_ref[...] * xs
        y_o[0] = _rms(yt * _silu(z_ref[0]), ng_ref[...]).astype(y_o.dtype)
    else:
        y_o[0] = y

    @pl.when(t == nc - 1)
    def _():
        h_o[0] = h_s[...]


def _ssd_pass(d, xbc, dt, dtt, lw, h0, fin=None):
    b, l, _ = xbc.shape
    q = SSD_CHUNK
    nc = l // q
    final = fin is not None
    cidx = (lambda t: nc - 1 - t) if d == 1 else (lambda t: t)
    r8 = q // 8
    nb8 = l // 8
    const = lambda shape: pl.BlockSpec(shape, lambda i, t: (0,) * len(shape))
    tok = lambda w: pl.BlockSpec((1, q, w), lambda i, t: (i, cidx(t), 0))
    hspec = pl.BlockSpec((1, SSM_HEADS, SSM_HEAD_DIM, SSM_STATE), lambda i, t: (i, 0, 0, 0))
    in_specs = [
        tok(XBC_DIM),
        pl.BlockSpec((1, 8, XBC_DIM), lambda i, t: (i, jnp.maximum(cidx(t) * r8 - 1, 0), 0)),
        pl.BlockSpec((1, 8, XBC_DIM), lambda i, t: (i, jnp.minimum((cidx(t) + 1) * r8, nb8 - 1), 0)),
        tok(V7X_LANES),
        pl.BlockSpec((1, 16, q), lambda i, t: (i, 0, cidx(t))),
        const((8, XBC_DIM)), const((1, XBC_DIM)), const((1, V7X_LANES)), const((16, 1)),
        const((1, V7X_LANES)), const((16, 1)), hspec]
    args = [xbc, xbc, xbc, dt, dtt, lw["conv_w"], lw["conv_b"], lw["dt_b"], lw["dt_bt"],
            lw["a_row"], lw["a_col"], h0]
    if final:
        in_specs += [tok(SSM_INNER), tok(SSM_INNER), const((1, SSM_INNER)), const((1, SSM_INNER))]
        args += [fin[0], fin[1], lw["d_skip"], lw["ssm_norm"]]
    return pl.pallas_call(
        functools.partial(_ssd_kernel, d, final),
        out_shape=[jax.ShapeDtypeStruct((b, l, SSM_INNER), BF16 if final else F32),
                   jax.ShapeDtypeStruct(h0.shape, F32)],
        grid=(b, nc),
        in_specs=in_specs,
        out_specs=[tok(SSM_INNER), hspec],
        scratch_shapes=[pltpu.VMEM((SSM_HEADS, SSM_HEAD_DIM, SSM_STATE), F32)],
        compiler_params=_cparams(("parallel", "arbitrary")),
        name=f"ssd_scan_d{d}{'_final' if final else ''}",
    )(*args)


def _merge_kernel(ya_ref, yb_ref, yc_ref, gt_ref, hx_ref, wa_ref, wb_ref, wc_ref, wo_ref, gm_ref,
                  npost_ref, npre_ref, shf_ref, scf_ref, wr_ref, hx_o, vx_o, aff_o):
    d = hx_ref.shape[2]
    gt = gt_ref[0]
    m = (jax.nn.sigmoid(gt[:, :d]) * _dot(ya_ref[0], wa_ref[...])
         + jax.nn.sigmoid(gt[:, d:2 * d]) * _dot(yb_ref[0], wb_ref[...])
         + jax.nn.sigmoid(gt[:, 2 * d:]) * _dot(yc_ref[0], wc_ref[...]))
    out = _dot(m.astype(BF16), wo_ref[...])
    hx = hx_ref[0] + gm_ref[0] * _rms(out, npost_ref[...])
    hx_o[0] = hx
    v = _rms(hx, npre_ref[...]) * (1.0 + scf_ref[0]) + shf_ref[0]
    vx_o[0] = v.astype(BF16)
    logits = _nt_dot(wr_ref[...], v, HIGHEST)
    e = jnp.exp(logits - jnp.max(logits, axis=0, keepdims=True))
    aff_o[0] = e / jnp.sum(e, axis=0, keepdims=True)


def _merge(ya, yb, yc, gates, hx, lw, gm, shf, scf):
    b, t, d = hx.shape
    tm = 256
    const = lambda shape: pl.BlockSpec(shape, lambda i, j: (0,) * len(shape))
    tok = lambda w: pl.BlockSpec((1, tm, w), lambda i, j: (i, j, 0))
    vec = pl.BlockSpec((1, 1, d), lambda i, j: (i, 0, 0))
    return pl.pallas_call(
        _merge_kernel,
        out_shape=[jax.ShapeDtypeStruct((b, t, d), F32), jax.ShapeDtypeStruct((b, t, d), BF16),
                   jax.ShapeDtypeStruct((b, N_EXPERTS, t), F32)],
        grid=(b, t // tm),
        in_specs=[tok(A_Q_DIM), tok(NA_DIM), tok(SSM_INNER), tok(N_BRANCH * d), tok(d),
                  const((A_Q_DIM, d)), const((NA_DIM, d)), const((SSM_INNER, d)), const((d, d)),
                  vec, const((1, d)), const((1, d)), vec, vec, const((N_EXPERTS, d))],
        out_specs=[tok(d), tok(d), pl.BlockSpec((1, N_EXPERTS, tm), lambda i, j: (i, 0, j))],
        compiler_params=_cparams(("parallel", "parallel")),
        name="merge_residual_router",
    )(ya, yb, yc, gates, hx, lw["w_a"], lw["w_b"], lw["w_c"], lw["w_out"], gm, lw["n_post"],
      lw["n_ffn_pre"], shf, scf, lw["w_rt"])


def _topk_kernel(cap, aff_ref, pos_o, gate_o, cs_o, eq_s, gt_s, ceq_s, csel_s):
    x = aff_ref[0]
    ne, t = x.shape
    bits = pltpu.bitcast(x, I32)
    theta = jnp.zeros((ne, 1), I32)
    for bit in range(30, -1, -1):
        cand = theta | (1 << bit)
        cnt = jnp.sum((bits >= cand).astype(I32), axis=-1, keepdims=True)
        theta = jnp.where(cnt >= cap, cand, theta)
    gt = bits > theta
    eq = bits == theta
    need = cap - jnp.sum(gt.astype(I32), axis=-1, keepdims=True)
    eq_s[...] = eq.astype(F32)
    gt_s[...] = gt.astype(F32)
    tb = ROUTE_TILE
    ii = lax.broadcasted_iota(I32, (tb, tb), 0)
    jj = lax.broadcasted_iota(I32, (tb, tb), 1)
    upper = (ii <= jj).astype(BF16)
    lane = lax.broadcasted_iota(I32, (ne, V7X_LANES), 1)
    needf = need.astype(F32)

    ceq_s[...] = jnp.zeros(ceq_s.shape, F32)
    csel_s[...] = jnp.zeros(csel_s.shape, F32)
    cs_o[0] = jnp.zeros((ne, V7X_LANES), I32)

    def body(k, carry):
        c_eq, c_sel = ceq_s[...], csel_s[...]
        sl = pl.ds(pl.multiple_of(k * tb, tb), tb)
        e_k = eq_s[:, sl]
        g_k = gt_s[:, sl]
        inc_eq = _dot(e_k.astype(BF16), upper)
        rank_eq = c_eq + inc_eq - e_k
        sel = g_k + e_k * (rank_eq < needf).astype(F32)
        inc_sel = _dot(sel.astype(BF16), upper)
        slot = c_sel + inc_sel - sel
        pos_o[0, :, sl] = jnp.where(sel > 0.5, slot, -1.0).astype(I32)
        gate_o[0, :, sl] = jnp.where(sel > 0.5, aff_ref[0, :, sl], 0.0)
        cs_o[0] = jnp.where(lane == k, c_sel.astype(I32), cs_o[0])
        ceq_s[...] = c_eq + inc_eq[:, tb - 1:tb]
        csel_s[...] = c_sel + inc_sel[:, tb - 1:tb]
        return carry

    nk = t // tb
    lax.fori_loop(0, nk, body, 0)
    cs_o[0] = jnp.where(lane == nk, csel_s[...].astype(I32), cs_o[0])


def _topk(aff_t, cap):
    b, ne, t = aff_t.shape
    assert t % ROUTE_TILE == 0 and t // ROUTE_TILE < V7X_LANES
    full = pl.BlockSpec((1, ne, t), lambda i: (i, 0, 0))
    return pl.pallas_call(
        functools.partial(_topk_kernel, cap),
        out_shape=[jax.ShapeDtypeStruct((b, ne, t), I32), jax.ShapeDtypeStruct((b, ne, t), F32),
                   jax.ShapeDtypeStruct((b, ne, V7X_LANES), I32)],
        grid=(b,),
        in_specs=[full],
        out_specs=[full, full, pl.BlockSpec((1, ne, V7X_LANES), lambda i: (i, 0, 0))],
        scratch_shapes=[pltpu.VMEM((ne, t), F32), pltpu.VMEM((ne, t), F32),
                        pltpu.VMEM((ne, 1), F32), pltpu.VMEM((ne, 1), F32)],
        compiler_params=_cparams(("parallel",)),
        name="expert_topk",
    )(aff_t)


def _cs_at(cs_ref, b, e, k):
    return cs_ref[(b * N_EXPERTS + e) * V7X_LANES + k]


_GATHER_SMALL = 64


def _gather_kernel(sb, nblk, cs_ref, pos_ref, vx_ref, xs_o, st_s):
    b, e, k = pl.program_id(0), pl.program_id(1), pl.program_id(2)
    c0 = _cs_at(cs_ref, b, e, k)
    c1 = _cs_at(cs_ref, b, e, k + 1)
    blk = jnp.minimum(c0 // sb, nblk - 1)
    blk_prev = jnp.minimum(_cs_at(cs_ref, b, e, jnp.maximum(k - 1, 0)) // sb, nblk - 1)
    blk_next = jnp.minimum(c1 // sb, nblk - 1)
    n_sel = c1 - c0

    @pl.when(k == 0)
    def _():
        st_s[...] = jnp.zeros(st_s.shape, F32)

    @pl.when(jnp.logical_and(k > 0, blk != blk_prev))
    def _():
        st_s[0:sb, :] = st_s[sb:2 * sb, :]
        st_s[sb:2 * sb, :] = jnp.zeros((sb, st_s.shape[1]), F32)

    tt = vx_ref.shape[1]
    local = pos_ref[0, 0] - blk * sb

    def place(w0, wn):
        onehot = (lax.broadcasted_iota(I32, (wn, tt), 0) + w0 == local).astype(BF16)
        st_s[pl.ds(w0, wn), :] += _dot(onehot, vx_ref[0])

    narrow = 2 * _GATHER_SMALL
    if 2 * sb > narrow:
        w0 = pl.multiple_of(((c0 - blk * sb) // _GATHER_SMALL) * _GATHER_SMALL, _GATHER_SMALL)
        pl.when(jnp.logical_and(n_sel > 0, n_sel <= _GATHER_SMALL))(lambda: place(w0, narrow))
        pl.when(n_sel > _GATHER_SMALL)(lambda: place(0, 2 * sb))
    else:
        pl.when(n_sel > 0)(lambda: place(0, 2 * sb))

    @pl.when(jnp.logical_or(k == pl.num_programs(2) - 1, blk_next != blk))
    def _():
        xs_o[0, 0] = st_s[0:sb, :].astype(xs_o.dtype)


def _gather(cs_flat, pos_t, vx, cap):
    b, t, d = vx.shape
    ne = N_EXPERTS
    tt = ROUTE_TILE
    sb = min(SLOT_BLOCK, cap)
    nblk = cap // sb
    blk_of = lambda i, e, k, cs: jnp.minimum(cs[(i * ne + e) * V7X_LANES + k] // sb, nblk - 1)
    grid_spec = pltpu.PrefetchScalarGridSpec(
        num_scalar_prefetch=1,
        grid=(b, ne, t // tt),
        in_specs=[pl.BlockSpec((1, 1, 1, tt), lambda i, e, k, cs: (i, e, 0, k)),
                  pl.BlockSpec((1, tt, d), lambda i, e, k, cs: (i, k, 0))],
        out_specs=pl.BlockSpec((1, 1, sb, d), lambda i, e, k, cs: (i, e, blk_of(i, e, k, cs), 0)),
        scratch_shapes=[pltpu.VMEM((2 * sb, d), F32)])
    return pl.pallas_call(
        functools.partial(_gather_kernel, sb, nblk),
        out_shape=jax.ShapeDtypeStruct((b, ne, cap, d), BF16),
        grid_spec=grid_spec,
        compiler_params=_cparams(("parallel", "parallel", "arbitrary")),
        name="expert_gather",
    )(cs_flat, pos_t.reshape(b, ne, 1, t), vx)


_FFN_TILE_F = 512


def _ffn_kernel(x_ref, wg_ref, wu_ref, wd_ref, y_o, acc_s):
    j = pl.program_id(2)
    x = x_ref[0, 0]
    hid = _silu(_dot(x, wg_ref[0, 0].astype(BF16))) * _dot(x, wu_ref[0, 0].astype(BF16))
    part = _dot(hid.astype(BF16), wd_ref[0, 0].astype(BF16))

    @pl.when(j == 0)
    def _():
        acc_s[...] = part

    @pl.when(j > 0)
    def _():
        acc_s[...] += part

    @pl.when(j == pl.num_programs(2) - 1)
    def _():
        y_o[0, 0] = acc_s[...].astype(y_o.dtype)


def _expert_ffn(xs, layer, wg, wu, wd):
    b, ne, cap, d = xs.shape
    f = wg.shape[3]
    tf = _FFN_TILE_F
    tok = pl.BlockSpec((1, 1, cap, d), lambda e, i, j: (i, e, 0, 0))
    return pl.pallas_call(
        _ffn_kernel,
        out_shape=jax.ShapeDtypeStruct(xs.shape, BF16),
        grid=(ne, b, f // tf),
        in_specs=[tok,
                  pl.BlockSpec((1, 1, d, tf), lambda e, i, j: (layer, e, 0, j)),
                  pl.BlockSpec((1, 1, d, tf), lambda e, i, j: (layer, e, 0, j)),
                  pl.BlockSpec((1, 1, tf, d), lambda e, i, j: (layer, e, j, 0))],
        out_specs=tok,
        scratch_shapes=[pltpu.VMEM((cap, d), F32)],
        compiler_params=_cparams(("parallel", "parallel", "arbitrary")),
        name="expert_ffn",
    )(xs, wg, wu, wd)


_SLOT_ALIGN = 16


def _win_start(c0, cap, win):
    return jnp.minimum((c0 // _SLOT_ALIGN) * _SLOT_ALIGN, cap - win)


def _combine_kernel(cap, win, win_fast, cs_ref, *refs):
    ne = N_EXPERTS
    ys = refs[:ne]
    pos_ref, gate_ref, hx_ref, gf_ref, npost_ref, hx_o = refs[ne:]
    b, k = pl.program_id(0), pl.program_id(1)
    tt, d = hx_ref.shape[1], hx_ref.shape[2]
    pos, gate = pos_ref[0], gate_ref[0]
    starts, fits = [], None
    for e in range(ne):
        c0 = _cs_at(cs_ref, b, e, k)
        a = _win_start(c0, cap, win)
        starts.append(a)
        fit = _cs_at(cs_ref, b, e, k + 1) - a <= win_fast
        fits = fit if fits is None else jnp.logical_and(fits, fit)

    def run(width):
        lane = lax.broadcasted_iota(I32, (tt, width), 1)
        acc = jnp.zeros((tt, d), F32)
        for e in range(ne):
            local = pos[:, e:e + 1] - starts[e]
            onehot = jnp.where(lane == local, gate[:, e:e + 1], 0.0).astype(BF16)
            acc = acc + _dot(onehot, ys[e][0, 0, :width, :])
        hx_o[0] = hx_ref[0] + gf_ref[0] * _rms(acc, npost_ref[...])

    if win_fast < win:
        pl.when(fits)(lambda: run(win_fast))
        pl.when(jnp.logical_not(fits))(lambda: run(win))
    else:
        run(win)


def _combine(cs_flat, ys, pos_r, gate_r, hx, gf, n_post):
    b, t, d = hx.shape
    ne, cap = ys.shape[1], ys.shape[2]
    tt = ROUTE_TILE
    win = min(tt + _SLOT_ALIGN, cap)
    win_fast = min(tt, win)
    assert (cap - win) % _SLOT_ALIGN == 0

    def window(e):
        return pl.BlockSpec(
            (pl.Element(1), pl.Element(1), pl.Element(win), pl.Element(d)),
            lambda i, k, cs: (i, e, pl.multiple_of(_win_start(cs[(i * ne + e) * V7X_LANES + k], cap, win), _SLOT_ALIGN), 0))

    tok = lambda w: pl.BlockSpec((1, tt, w), lambda i, k, cs: (i, k, 0))
    grid_spec = pltpu.PrefetchScalarGridSpec(
        num_scalar_prefetch=1,
        grid=(b, t // tt),
        in_specs=[window(e) for e in range(ne)] + [
            tok(ne), tok(ne), tok(d),
            pl.BlockSpec((1, 1, d), lambda i, k, cs: (i, 0, 0)),
            pl.BlockSpec((1, d), lambda i, k, cs: (0, 0))],
        out_specs=tok(d))
    return pl.pallas_call(
        functools.partial(_combine_kernel, cap, win, win_fast),
        out_shape=jax.ShapeDtypeStruct((b, t, d), F32),
        grid_spec=grid_spec,
        compiler_params=_cparams(("parallel", "parallel")),
        name="expert_combine",
    )(cs_flat, *([ys] * ne), pos_r, gate_r, hx, gf, n_post)


def _moe(vx, aff_t, hx, gf, lw):
    b, t, d = hx.shape
    cap = EC_CAPACITY * t // N_EXPERTS
    pos_t, gate_t, cs = _topk(aff_t, cap)
    cs_flat = cs.reshape(-1)
    xs = _gather(cs_flat, pos_t, vx, cap)
    ys = _expert_ffn(xs, lw["layer"], lw["w_gate"], lw["w_up"], lw["w_down"])
    return _combine(cs_flat, ys, pos_t.transpose(0, 2, 1), gate_t.transpose(0, 2, 1), hx, gf,
                    lw["n_ffn_post"])


def _rope_tables(s):
    t = jnp.arange(s, dtype=I32)
    rows = (t // GRID_W).astype(F32)
    cols = (t % GRID_W).astype(F32)
    n_freq = HEAD_DIM // 4
    inv = ROPE_BASE ** (-jnp.arange(n_freq, dtype=F32) / n_freq)
    ar = rows[:, None] * inv[None, :]
    ac = cols[:, None] * inv[None, :]
    ang = jnp.concatenate([ar, ar, ac, ac] * (V7X_LANES // HEAD_DIM), axis=-1)
    return jnp.cos(ang), jnp.sin(ang)


def _layer_weights(i, p):
    d = p["w_in"].shape[1]
    w_in = p["w_in"][i]
    dt_pad = jnp.zeros((d, V7X_LANES - N_DIRS * SSM_HEADS), F32)
    hm = np.kron(np.eye(A_HEADS, dtype=np.float32), np.full((HEAD_DIM, HEAD_DIM), 1.0 / HEAD_DIM, np.float32))
    lane_pad = lambda v: jnp.pad(v.reshape(1, -1), ((0, 0), (0, V7X_LANES - v.size)))
    a = -jnp.exp(p["ssm_a_log"][i].astype(F32)).reshape(-1)
    return {
        "w_main": jnp.concatenate([w_in[:, :_DT_COL], w_in[:, _DT_COL + N_DIRS * SSM_HEADS:]], axis=1).astype(BF16),
        "w_dt": jnp.concatenate([w_in[:, _DT_COL:_DT_COL + N_DIRS * SSM_HEADS], dt_pad], axis=1).astype(BF16),
        "qn": jnp.tile(p["q_norm"][i], A_HEADS).reshape(1, -1),
        "kn": jnp.tile(p["k_norm"][i], A_KV_HEADS).reshape(1, -1),
        "hm": jnp.asarray(hm, BF16),
        "n_pre": p["norm_mix_pre"][i].reshape(1, d),
        "n_post": p["norm_mix_post"][i].reshape(1, d),
        "n_ffn_pre": p["norm_ffn_pre"][i].reshape(1, d),
        "n_ffn_post": p["norm_ffn_post"][i].reshape(1, d),
        "bias_tab": _na_bias_table(p["na_rel_bias"][i]),
        "conv_w": jnp.pad(p["conv_w"][i], ((0, 8 - SSM_CONV), (0, 0))),
        "conv_b": p["conv_b"][i].reshape(1, -1),
        "dt_b": lane_pad(p["ssm_dt_bias"][i]),
        "dt_bt": p["ssm_dt_bias"][i].reshape(-1, 1),
        "a_row": lane_pad(a),
        "a_col": a.reshape(-1, 1),
        "d_skip": jnp.repeat(p["ssm_d"][i, 0] + p["ssm_d"][i, 1], SSM_HEAD_DIM).reshape(1, -1),
        "ssm_norm": p["ssm_norm"][i].reshape(1, -1),
        "w_a": p["w_branch_a"][i].astype(BF16),
        "w_b": p["w_branch_b"][i].astype(BF16),
        "w_c": p["w_branch_c"][i].astype(BF16),
        "w_out": p["w_out"][i].astype(BF16),
        "w_rt": p["w_router"][i].T,
        "layer": i,
        "w_gate": p["w_exp_gate"],
        "w_up": p["w_exp_up"],
        "w_down": p["w_exp_down"],
    }


def _ssd_branch(px, pc, lw, with_ctx_out):
    zx, xbcx, dtx = px
    zc, xbcc, dtc = pc
    b = zx.shape[0]
    nd = N_DIRS * SSM_HEADS
    dttx = dtx[:, :, :nd].transpose(0, 2, 1)
    dttc = dtc[:, :, :nd].transpose(0, 2, 1)
    h0 = jnp.zeros((b, SSM_HEADS, SSM_HEAD_DIM, SSM_STATE), F32)
    y0c, hc0 = _ssd_pass(0, xbcc, dtc, dttc, lw, h0)
    yc, hc1 = _ssd_pass(1, xbcc, dtc, dttc, lw, h0, fin=(y0c, zc))
    y0x, _ = _ssd_pass(0, xbcx, dtx, dttx, lw, hc0)
    yx, _ = _ssd_pass(1, xbcx, dtx, dttx, lw, hc1, fin=(y0x, zx))
    return yx, (yc if with_ctx_out else None)


def _layer(i, hx, hc, mod_x, mod_c, p, rope_tabs, last):
    lw = _layer_weights(i, p)
    d = hx.shape[2]
    sh_m, sc_m, g_m, sh_f, sc_f, g_f = [mod_x[:, None, j * d:(j + 1) * d] for j in range(N_ADA)]
    csh_m, csc_m, cg_m, csh_f, csc_f, cg_f = [mod_c[:, None, j * d:(j + 1) * d] for j in range(N_ADA)]

    qa_c, ka_c, va_c, qb_c, kb_c, vb_c, z_c, xbc_c, dt_c, gt_c = _proj(hc, lw["n_pre"], csh_m, csc_m, lw, None)
    qa, ka, va, qb, kb, vb, z, xbc, dt, gt = _proj(hx, lw["n_pre"], sh_m, sc_m, lw, rope_tabs)

    ya = _flash(qa, jnp.concatenate([ka, ka_c], axis=1), jnp.concatenate([va, va_c], axis=1),
                A_KV_HEADS, A_HEADS // A_KV_HEADS, True)
    yb = _na(qb, kb, vb, kb_c, vb_c, lw["bias_tab"])
    yc, yc_c = _ssd_branch((z, xbc, dt), (z_c, xbc_c, dt_c), lw, not last)
    hx, vx, aff = _merge(ya, yb, yc, gt, hx, lw, g_m, sh_f, sc_f)
    hx = _moe(vx, aff, hx, g_f, lw)
    if last:
        return hx, hc
    ya_c = _flash(qa_c, ka_c, va_c, A_KV_HEADS, A_HEADS // A_KV_HEADS, True)
    yb_c = _flash(qb_c, kb_c, vb_c, NA_HEADS, 1, False)
    hc, vc, aff_c = _merge(ya_c, yb_c, yc_c, gt_c, hc, lw, cg_m, csh_f, csc_f)
    hc = _moe(vc, aff_c, hc, cg_f, lw)
    return hx, hc


def kernel(x, c, ctx, c_ctx, w_ada, b_ada, norm_mix_pre, norm_mix_post, norm_ffn_pre, norm_ffn_post, w_in, q_norm, k_norm, na_rel_bias, conv_w, conv_b, ssm_a_log, ssm_dt_bias, ssm_d, ssm_norm, w_branch_a, w_branch_b, w_branch_c, w_out, w_router, w_exp_gate, w_exp_up, w_exp_down):
    p = dict(norm_mix_pre=norm_mix_pre, norm_mix_post=norm_mix_post, norm_ffn_pre=norm_ffn_pre,
             norm_ffn_post=norm_ffn_post, w_in=w_in, q_norm=q_norm, k_norm=k_norm, na_rel_bias=na_rel_bias,
             conv_w=conv_w, conv_b=conv_b, ssm_a_log=ssm_a_log, ssm_dt_bias=ssm_dt_bias, ssm_d=ssm_d,
             ssm_norm=ssm_norm, w_branch_a=w_branch_a, w_branch_b=w_branch_b, w_branch_c=w_branch_c,
             w_out=w_out, w_router=w_router, w_exp_gate=w_exp_gate, w_exp_up=w_exp_up, w_exp_down=w_exp_down)
    b, s, d = x.shape
    depth = w_ada.shape[0]
    assert b + 1 <= 8
    cc = jnp.concatenate([c, c_ctx[None, :], jnp.zeros((8 - b - 1, d), F32)], axis=0)
    mod = _ada_mod(cc, w_ada, b_ada)
    rope_tabs = _rope_tables(s)
    hx, hc = x, ctx
    for i in range(depth):
        mod_x = mod[i, :b]
        mod_c = jnp.broadcast_to(mod[i, b:b + 1], (b, mod.shape[2]))
        hx, hc = _layer(i, hx, hc, mod_x, mod_c, p, rope_tabs, i == depth - 1)
    return hx
```

```python
import functools

import jax
import jax.numpy as jnp
import numpy as np
from jax import lax
from jax.experimental import pallas as pl
from jax.experimental.pallas import tpu as pltpu

F32, BF16, I32 = jnp.float32, jnp.bfloat16, jnp.int32
HIGHEST = lax.Precision.HIGHEST

GRID_W = 64
HEAD_DIM = 64
ROPE_BASE = 10000.0
NORM_EPS = 1e-6
N_ADA = 6
A_HEADS, A_KV_HEADS = 8, 2
NA_HEADS, NA_KH, NA_KW = 8, 8, 16
SSM_HEADS, SSM_HEAD_DIM, SSM_INNER = 8, 64, 512
SSM_GROUPS, SSM_STATE, SSM_CONV, N_DIRS = 2, 64, 5, 2
XBC_DIM = SSM_INNER + 2 * SSM_GROUPS * SSM_STATE
A_Q_DIM, A_KV_DIM, NA_DIM = A_HEADS * HEAD_DIM, A_KV_HEADS * HEAD_DIM, NA_HEADS * HEAD_DIM
N_EXPERTS, EC_CAPACITY = 16, 2
N_BRANCH = 3
ATT_SCALE = HEAD_DIM ** -0.5
LOG2_E = 1.4426950408889634

V7X_LANES = 128
V7X_VMEM_LIMIT_BYTES = 56 * 1024 * 1024
NEG_BIG = -1e30

SSD_CHUNK = 128
ROUTE_TILE = 256
SLOT_BLOCK = 256


def _cparams(sem):
    return pltpu.CompilerParams(dimension_semantics=sem, vmem_limit_bytes=V7X_VMEM_LIMIT_BYTES)


def _nt_dot(a, b, precision=None):
    return lax.dot_general(a, b, (((1,), (1,)), ((), ())), preferred_element_type=F32,
                           precision=precision)


def _dot(a, b, precision=None):
    return jnp.dot(a, b, preferred_element_type=F32, precision=precision)


def _silu(x):
    return x * jax.nn.sigmoid(x)


def _rms(x, gain):
    return x * lax.rsqrt(jnp.mean(x * x, axis=-1, keepdims=True) + NORM_EPS) * gain


def _ada_kernel(c_ref, w_ref, b_ref, o_ref):
    o_ref[0] = _dot(_silu(c_ref[...]), w_ref[0], HIGHEST) + b_ref[0]


def _ada_mod(cc, w_ada, b_ada):
    nl, d, n = w_ada.shape
    tn = 512
    return pl.pallas_call(
        _ada_kernel,
        out_shape=jax.ShapeDtypeStruct((nl, 8, n), F32),
        grid=(nl, n // tn),
        in_specs=[pl.BlockSpec((8, d), lambda l, j: (0, 0)),
                  pl.BlockSpec((1, d, tn), lambda l, j: (l, 0, j)),
                  pl.BlockSpec((1, 1, tn), lambda l, j: (l, 0, j))],
        out_specs=pl.BlockSpec((1, 8, tn), lambda l, j: (l, 0, j)),
        compiler_params=_cparams(("parallel", "parallel")),
        name="ada_mod",
    )(cc, w_ada, b_ada.reshape(nl, 1, n))


_OFF_QA, _OFF_KA, _OFF_VA, _OFF_QB, _OFF_KB, _OFF_VB, _OFF_Z, _OFF_XBC, _OFF_G, _OFF_END = (
    0, 512, 640, 768, 1280, 1792, 2304, 2816, 3584, 6656)
_DT_COL = 3584


def _head_rms(x, hm, gain):
    ms = _dot((x * x).astype(BF16), hm)
    return x * lax.rsqrt(ms + NORM_EPS) * gain


def _rope(x, cos, sin):
    w = x.shape[-1]
    lane = lax.broadcasted_iota(I32, x.shape, 1)
    first = (lane & 16) == 0
    rot = jnp.where(first, -pltpu.roll(x, w - 16, 1), pltpu.roll(x, 16, 1))
    return x * cos + rot * sin


def _proj_kernel(rope, h_ref, g_ref, sh_ref, sc_ref, w_ref, wdt_ref, qn_ref, kn_ref, hm_ref, *rest):
    if rope:
        cos_ref, sin_ref = rest[:2]
        rest = rest[2:]
    qa_o, ka_o, va_o, qb_o, kb_o, vb_o, z_o, xbc_o, dt_o, gt_o = rest
    u = _rms(h_ref[0], g_ref[...]) * (1.0 + sc_ref[0]) + sh_ref[0]
    ub = u.astype(BF16)

    def mm(lo, hi):
        return _dot(ub, w_ref[:, lo:hi])

    qa = _head_rms(mm(_OFF_QA, _OFF_KA), hm_ref[...], qn_ref[...])
    ka = _head_rms(mm(_OFF_KA, _OFF_VA), hm_ref[:A_KV_DIM, :A_KV_DIM], kn_ref[...])
    if rope:
        cs, sn = cos_ref[...], sin_ref[...]
        qa = _rope(qa, jnp.concatenate([cs] * 4, axis=1), jnp.concatenate([sn] * 4, axis=1))
        ka = _rope(ka, cs, sn)
    qa_o[0] = (qa * (ATT_SCALE * LOG2_E)).astype(BF16)
    ka_o[0] = ka.astype(BF16)
    va_o[0] = mm(_OFF_VA, _OFF_QB).astype(BF16)
    qb_o[0] = (mm(_OFF_QB, _OFF_KB) * ATT_SCALE).astype(BF16)
    kb_o[0] = mm(_OFF_KB, _OFF_VB).astype(BF16)
    vb_o[0] = mm(_OFF_VB, _OFF_Z).astype(BF16)
    z_o[0] = mm(_OFF_Z, _OFF_XBC)
    xbc_o[0] = mm(_OFF_XBC, _OFF_G)
    dt_o[0] = _dot(ub, wdt_ref[...])
    for i in range((_OFF_END - _OFF_G) // 512):
        gt_o[0, :, i * 512:(i + 1) * 512] = mm(_OFF_G + i * 512, _OFF_G + (i + 1) * 512)


def _proj(h, gain, shift, scale, lw, rope_tabs):
    b, t, d = h.shape
    tm = 256
    rope = rope_tabs is not None
    const = lambda shape: pl.BlockSpec(shape, lambda i, j: (0,) * len(shape))
    tok = lambda w: pl.BlockSpec((1, tm, w), lambda i, j: (i, j, 0))
    vec = pl.BlockSpec((1, 1, d), lambda i, j: (i, 0, 0))
    in_specs = [tok(d), const((1, d)), vec, vec, const((d, _OFF_END)), const((d, V7X_LANES)),
                const((1, A_Q_DIM)), const((1, A_KV_DIM)), const((A_Q_DIM, A_Q_DIM))]
    args = [h, gain, shift, scale, lw["w_main"], lw["w_dt"], lw["qn"], lw["kn"], lw["hm"]]
    if rope:
        tab = pl.BlockSpec((tm, V7X_LANES), lambda i, j: (j, 0))
        in_specs += [tab, tab]
        args += list(rope_tabs)
    widths = [(A_Q_DIM, BF16), (A_KV_DIM, BF16), (A_KV_DIM, BF16), (NA_DIM, BF16), (NA_DIM, BF16),
              (NA_DIM, BF16), (SSM_INNER, F32), (XBC_DIM, F32), (V7X_LANES, F32),
              (_OFF_END - _OFF_G, F32)]
    return pl.pallas_call(
        functools.partial(_proj_kernel, rope),
        out_shape=[jax.ShapeDtypeStruct((b, t, w), dt) for w, dt in widths],
        grid=(b, t // tm),
        in_specs=in_specs,
        out_specs=[tok(w) for w, _ in widths],
        compiler_params=_cparams(("parallel", "parallel")),
        name="norm_mod_proj",
    )(*args)


def _flash_kernel(nkv, grp, base2, q_ref, kt_ref, v_ref, o_ref, q_s, m_s, acc_s):
    j = pl.program_id(2)
    tq = q_ref.shape[1]
    tk = kt_ref.shape[2]
    hd = HEAD_DIM
    ex = jnp.exp2 if base2 else jnp.exp

    @pl.when(j == 0)
    def _():
        qb = q_ref[0]
        for g in range(nkv):
            parts = [qb[:, (g * grp + i) * hd:(g * grp + i + 1) * hd] for i in range(grp)]
            q_s[g] = parts[0] if grp == 1 else jnp.concatenate(parts, axis=0)
        m_s[...] = jnp.full(m_s.shape, -jnp.inf, F32)
        acc_s[...] = jnp.zeros(acc_s.shape, F32)

    for g in range(nkv):
        s = _dot(q_s[g], kt_ref[0, g * hd:(g + 1) * hd, :])
        m_prev = m_s[g]
        m_new = jnp.maximum(m_prev, jnp.max(s, axis=-1, keepdims=True))
        alpha = ex(m_prev - m_new)
        m_s[g] = m_new
        p = jnp.concatenate(
            [ex(s[:, c * V7X_LANES:(c + 1) * V7X_LANES] - m_new).astype(BF16) for c in range(tk // V7X_LANES)],
            axis=1)
        acc_s[g] = alpha * acc_s[g] + _dot(p, v_ref[0, :, g * V7X_LANES:(g + 1) * V7X_LANES])

    @pl.when(j == pl.num_programs(2) - 1)
    def _():
        for g in range(nkv):
            acc = acc_s[g]
            o = acc / pltpu.roll(acc, hd, 1)
            for i in range(grp):
                c0 = (g * grp + i) * hd
                o_ref[0, :, c0:c0 + hd] = o[i * tq:(i + 1) * tq, :hd].astype(o_ref.dtype)


def _pick_tile(n, cands):
    for c in cands:
        if n % c == 0:
            return c
    raise ValueError(f"no tile for {n}")


def _flash(q, k, v, nkv, grp, base2):
    b, s, qd = q.shape
    sk = k.shape[1]
    kt = k.transpose(0, 2, 1)
    ones = jnp.ones((b, sk, nkv, V7X_LANES - HEAD_DIM), v.dtype)
    vaug = jnp.concatenate([v.reshape(b, sk, nkv, HEAD_DIM), ones], axis=-1).reshape(b, sk, nkv * V7X_LANES)
    tq = _pick_tile(s, (256, 128))
    tk = _pick_tile(sk, (3328, 1280, 1024, 512, 256, 128))
    rows = grp * tq
    return pl.pallas_call(
        functools.partial(_flash_kernel, nkv, grp, base2),
        out_shape=jax.ShapeDtypeStruct((b, s, qd), BF16),
        grid=(b, s // tq, sk // tk),
        in_specs=[pl.BlockSpec((1, tq, qd), lambda i, m, j: (i, m, 0)),
                  pl.BlockSpec((1, nkv * HEAD_DIM, tk), lambda i, m, j: (i, 0, j)),
                  pl.BlockSpec((1, tk, nkv * V7X_LANES), lambda i, m, j: (i, j, 0))],
        out_specs=pl.BlockSpec((1, tq, qd), lambda i, m, j: (i, m, 0)),
        scratch_shapes=[pltpu.VMEM((nkv, rows, HEAD_DIM), BF16),
                        pltpu.VMEM((nkv, rows, V7X_LANES), F32),
                        pltpu.VMEM((nkv, rows, V7X_LANES), F32)],
        compiler_params=_cparams(("parallel", "parallel", "arbitrary")),
        name=f"flash_attn_{nkv}x{grp}",
    )(q, kt, vaug)


_NA_ROWS_PER_STEP = 8
_NA_BLOCK = _NA_ROWS_PER_STEP * GRID_W
_NA_BAND = NA_KH * GRID_W


def _na_kernel(rows, q_ref, kp_ref, kc_ref, kn_ref, vp_ref, vc_ref, vn_ref, kx_ref, vx_ref, bias_ref,
               o_ref, kcat_s, vcat_s):
    g = pl.program_id(1)
    nb = _NA_BLOCK
    for i, (kr, vr) in enumerate(((kp_ref, vp_ref), (kc_ref, vc_ref), (kn_ref, vn_ref))):
        kcat_s[i * nb:(i + 1) * nb, :] = kr[0]
        vcat_s[i * nb:(i + 1) * nb, :] = vr[0]
    n_ctx = kx_ref.shape[1]
    lanes = V7X_LANES
    head0 = lax.broadcasted_iota(I32, (GRID_W, lanes), 1) < HEAD_DIM
    ones_loc = jnp.ones((_NA_BAND, lanes), BF16)
    ones_ctx = jnp.ones((n_ctx, lanes), BF16)

    def body(j, carry):
        r = g * _NA_ROWS_PER_STEP + j
        r0 = jnp.clip(r - NA_KH // 2, 0, rows - NA_KH)
        off = pl.multiple_of((r0 - g * _NA_ROWS_PER_STEP + _NA_ROWS_PER_STEP) * GRID_W, GRID_W)
        case = r - r0
        qrows = pl.ds(pl.multiple_of(j * GRID_W, GRID_W), GRID_W)
        for pr in range(NA_HEADS // 2):
            ls = slice(pr * lanes, (pr + 1) * lanes)
            qp = q_ref[0, qrows, ls]
            zero = jnp.zeros_like(qp)
            q2 = jnp.concatenate([jnp.where(head0, qp, zero), jnp.where(head0, zero, qp)], axis=0)
            kband = kcat_s[pl.ds(off, _NA_BAND), ls]
            vband = vcat_s[pl.ds(off, _NA_BAND), ls]
            s_loc = _nt_dot(q2, kband) + bias_ref[case * (NA_HEADS // 2) + pr]
            s_ctx = _nt_dot(q2, kx_ref[0, :, ls])
            m = jnp.maximum(jnp.max(s_loc, axis=-1, keepdims=True), jnp.max(s_ctx, axis=-1, keepdims=True))
            p_loc = jnp.exp(s_loc - m).astype(BF16)
            p_ctx = jnp.exp(s_ctx - m).astype(BF16)
            pvl = (_dot(p_loc, jnp.concatenate([vband, ones_loc], axis=1))
                   + _dot(p_ctx, jnp.concatenate([vx_ref[0, :, ls], ones_ctx], axis=1)))
            o2 = pvl[:, :lanes] / pvl[:, lanes:]
            o_ref[0, qrows, ls] = jnp.where(head0, o2[:GRID_W], o2[GRID_W:]).astype(o_ref.dtype)
        return carry

    lax.fori_loop(0, _NA_ROWS_PER_STEP, body, 0, unroll=2)


def _na_bias_table(rel_bias):
    nh, nr, nc = rel_bias.shape
    col = np.arange(GRID_W)
    c0 = np.clip(col - NA_KW // 2, 0, GRID_W - NA_KW)
    valid = (col[None, :] >= c0[:, None]) & (col[None, :] < c0[:, None] + NA_KW)
    w2 = 2 * GRID_W
    lo = GRID_W - NA_KW
    padded = jnp.pad(rel_bias.astype(F32), ((0, 0), (0, 0), (lo, w2 - lo - nc)))
    flat = jnp.broadcast_to(padded[:, :, None, :], (nh, nr, GRID_W, w2)).reshape(nh, nr, GRID_W * w2)
    skew = flat[:, :, :GRID_W * (w2 - 1)].reshape(nh, nr, GRID_W, w2 - 1)
    toe = skew[:, :, :, GRID_W - 1:]
    tab = jnp.stack([toe[:, NA_KH - 1 - case:2 * NA_KH - 1 - case] for case in range(NA_KH)])
    tab = jnp.where(valid[None, None, None], tab, NEG_BIG)
    tab = tab.transpose(0, 1, 3, 2, 4)
    return tab.reshape(NA_KH * NA_HEADS // 2, 2 * GRID_W, NA_KH * GRID_W)


def _na(q, k, v, kx, vx, bias_tab):
    b, s, d = q.shape
    l = kx.shape[1]
    rows = s // GRID_W
    assert rows % _NA_ROWS_PER_STEP == 0 and rows >= NA_KH
    ng = rows // _NA_ROWS_PER_STEP
    nb = _NA_BLOCK
    cur = pl.BlockSpec((1, nb, d), lambda i, g: (i, g, 0))
    prv = pl.BlockSpec((1, nb, d), lambda i, g: (i, jnp.maximum(g - 1, 0), 0))
    nxt = pl.BlockSpec((1, nb, d), lambda i, g: (i, jnp.minimum(g + 1, ng - 1), 0))
    ctx = pl.BlockSpec((1, l, d), lambda i, g: (i, 0, 0))
    return pl.pallas_call(
        functools.partial(_na_kernel, rows),
        out_shape=jax.ShapeDtypeStruct((b, s, d), BF16),
        grid=(b, ng),
        in_specs=[cur, prv, cur, nxt, prv, cur, nxt, ctx, ctx,
                  pl.BlockSpec(bias_tab.shape, lambda i, g: (0, 0, 0))],
        out_specs=cur,
        scratch_shapes=[pltpu.VMEM((3 * nb, d), BF16), pltpu.VMEM((3 * nb, d), BF16)],
        compiler_params=_cparams(("parallel", "parallel")),
        name="neighbourhood_attn",
    )(q, k, k, k, v, v, v, kx, vx, bias_tab)


def _softplus(x):
    return jnp.maximum(x, 0.0) + jnp.log1p(jnp.exp(-jnp.abs(x)))


def _ssd_kernel(d, final, xc_ref, xp_ref, xn_ref, dt_ref, dtt_ref, cw_ref, cb_ref, dtb_ref, dtbt_ref,
                a_ref, at_ref, h0_ref, *rest):
    if final:
        y0_ref, z_ref, dsk_ref, ng_ref, y_o, h_o, h_s, xcat_s = rest
    else:
        y0_ref = z_ref = dsk_ref = ng_ref = None
        y_o, h_o, h_s, xcat_s = rest
    t = pl.program_id(0)
    nc = pl.num_programs(0)
    reverse = d == 1
    c = nc - 1 - t if reverse else t
    q = xc_ref.shape[1]

    @pl.when(t == 0)
    def _():
        h_s[...] = h0_ref[...]

    pm = (c > 0).astype(F32)
    nm = (c < nc - 1).astype(F32)
    ii = lax.broadcasted_iota(I32, (q, q), 0)
    jj = lax.broadcasted_iota(I32, (q, q), 1)
    keep = (jj >= ii) if reverse else (jj <= ii)
    tri = keep.astype(F32)
    chan = lax.broadcasted_iota(I32, (V7X_LANES, SSM_INNER), 1)
    dt_col = lax.broadcasted_iota(I32, (V7X_LANES, SSM_INNER), 0)
    expand = (dt_col == d * SSM_HEADS + lax.shift_right_logical(chan, 6)).astype(F32)
    for bi in range(xc_ref.shape[0]):
        _ssd_chunk(d, bi, pm, nm, keep, tri, expand, xc_ref, xp_ref, xn_ref, dt_ref, dtt_ref, cw_ref, cb_ref,
                   dtb_ref, dtbt_ref, a_ref, at_ref, y0_ref, z_ref, dsk_ref, ng_ref, y_o, h_s, xcat_s)

    @pl.when(t == nc - 1)
    def _():
        h_o[...] = h_s[...]


def _ssd_chunk(d, bi, pm, nm, keep, tri, expand, xc_ref, xp_ref, xn_ref, dt_ref, dtt_ref, cw_ref, cb_ref,
               dtb_ref, dtbt_ref, a_ref, at_ref, y0_ref, z_ref, dsk_ref, ng_ref, y_o, h_s, xcat_s):
    final = y0_ref is not None
    reverse = d == 1
    q = xc_ref.shape[1]
    hp = SSM_HEAD_DIM
    ns = SSM_STATE

    xcat_s[bi, 0:8, :] = xp_ref[bi] * pm
    xcat_s[bi, 8:8 + q, :] = xc_ref[bi]
    xcat_s[bi, 8 + q:16 + q, :] = xn_ref[bi] * nm
    acc = jnp.zeros((q, XBC_DIM), F32)
    for k in range(SSM_CONV):
        acc = acc + xcat_s[bi, 8 - SSM_CONV // 2 + k:8 - SSM_CONV // 2 + k + q, :] * cw_ref[k:k + 1, :]
    u = _silu(acc + cb_ref[...])
    xs = u[:, :SSM_INNER]
    bm = u[:, SSM_INNER:SSM_INNER + SSM_GROUPS * ns]
    cm = u[:, SSM_INNER + SSM_GROUPS * ns:]

    dt = _softplus(dt_ref[bi] + dtb_ref[...])
    dtt = _softplus(dtt_ref[bi] + dtbt_ref[...])
    acs = _dot(tri, dt * a_ref[...], HIGHEST)
    acst = _nt_dot(dtt * at_ref[...], tri, HIGHEST)
    last = 0 if reverse else q - 1

    xdt_all = xs * _dot(dt, expand, HIGHEST)
    xdt_t = [xdt_all[:, p * V7X_LANES:(p + 1) * V7X_LANES].T for p in range(SSM_INNER // V7X_LANES)]
    xdt_b = xdt_all.astype(BF16)
    gmats = []
    for gi in range(SSM_GROUPS):
        gmats.append(_nt_dot(cm[:, gi * ns:(gi + 1) * ns].astype(BF16),
                             bm[:, gi * ns:(gi + 1) * ns].astype(BF16)))
    ys = []
    for h in range(SSM_HEADS):
        col = d * SSM_HEADS + h
        gi = h // (SSM_HEADS // SSM_GROUPS)
        a_col = acs[:, col:col + 1]
        a_row = acst[col:col + 1, :]
        tot = a_col[last:last + 1, :]
        lmat = jnp.exp(jnp.where(keep, a_col - a_row, NEG_BIG))
        y_diag = _dot((gmats[gi] * lmat).astype(BF16), xdt_b[:, h * hp:(h + 1) * hp])
        hst = h_s[bi, h]
        cg = cm[:, gi * ns:(gi + 1) * ns]
        bg = bm[:, gi * ns:(gi + 1) * ns]
        y_off = _nt_dot((cg * jnp.exp(a_col)).astype(BF16), hst.astype(BF16))
        ys.append(y_diag + y_off)
        bdec = (bg * jnp.exp(tot - a_col)).astype(BF16)
        xt = xdt_t[h // 2][(h % 2) * hp:(h % 2 + 1) * hp, :]
        h_s[bi, h] = jnp.exp(tot) * hst + _dot(xt.astype(BF16), bdec)
    y = jnp.concatenate(ys, axis=1)

    if final:
        yt = y0_ref[bi] + y + dsk_ref[...] * xs
        y_o[bi] = _rms(yt * _silu(z_ref[bi]), ng_ref[...]).astype(y_o.dtype)
    else:
        y_o[bi] = y


def _ssd_pass(d, xbc, dt, dtt, lw, h0, fin=None):
    b, l, _ = xbc.shape
    q = SSD_CHUNK
    nc = l // q
    final = fin is not None
    cidx = (lambda t: nc - 1 - t) if d == 1 else (lambda t: t)
    r8 = q // 8
    nb8 = l // 8
    const = lambda shape: pl.BlockSpec(shape, lambda t: (0,) * len(shape))
    tok = lambda w: pl.BlockSpec((b, q, w), lambda t: (0, cidx(t), 0))
    hspec = pl.BlockSpec((b, SSM_HEADS, SSM_HEAD_DIM, SSM_STATE), lambda t: (0, 0, 0, 0))
    in_specs = [
        tok(XBC_DIM),
        pl.BlockSpec((b, 8, XBC_DIM), lambda t: (0, jnp.maximum(cidx(t) * r8 - 1, 0), 0)),
        pl.BlockSpec((b, 8, XBC_DIM), lambda t: (0, jnp.minimum((cidx(t) + 1) * r8, nb8 - 1), 0)),
        tok(V7X_LANES),
        pl.BlockSpec((b, 16, q), lambda t: (0, 0, cidx(t))),
        const((8, XBC_DIM)), const((1, XBC_DIM)), const((1, V7X_LANES)), const((16, 1)),
        const((1, V7X_LANES)), const((16, 1)), hspec]
    args = [xbc, xbc, xbc, dt, dtt, lw["conv_w"], lw["conv_b"], lw["dt_b"], lw["dt_bt"],
            lw["a_row"], lw["a_col"], h0]
    if final:
        in_specs += [tok(SSM_INNER), tok(SSM_INNER), const((1, SSM_INNER)), const((1, SSM_INNER))]
        args += [fin[0], fin[1], lw["d_skip"], lw["ssm_norm"]]
    return pl.pallas_call(
        functools.partial(_ssd_kernel, d, final),
        out_shape=[jax.ShapeDtypeStruct((b, l, SSM_INNER), BF16 if final else F32),
                   jax.ShapeDtypeStruct(h0.shape, F32)],
        grid=(nc,),
        in_specs=in_specs,
        out_specs=[tok(SSM_INNER), hspec],
        scratch_shapes=[pltpu.VMEM((b, SSM_HEADS, SSM_HEAD_DIM, SSM_STATE), F32),
                        pltpu.VMEM((b, q + 16, XBC_DIM), F32)],
        compiler_params=_cparams(("arbitrary",)),
        name=f"ssd_scan_d{d}{'_final' if final else ''}",
    )(*args)


def _merge_kernel(ya_ref, yb_ref, yc_ref, gt_ref, hx_ref, wa_ref, wb_ref, wc_ref, wo_ref, gm_ref,
                  npost_ref, npre_ref, shf_ref, scf_ref, wr_ref, hx_o, vx_o, aff_o):
    d = hx_ref.shape[2]
    gt = gt_ref[0]
    m = (jax.nn.sigmoid(gt[:, :d]) * _dot(ya_ref[0], wa_ref[...])
         + jax.nn.sigmoid(gt[:, d:2 * d]) * _dot(yb_ref[0], wb_ref[...])
         + jax.nn.sigmoid(gt[:, 2 * d:]) * _dot(yc_ref[0], wc_ref[...]))
    out = _dot(m.astype(BF16), wo_ref[...])
    hx = hx_ref[0] + gm_ref[0] * _rms(out, npost_ref[...])
    hx_o[0] = hx
    v = _rms(hx, npre_ref[...]) * (1.0 + scf_ref[0]) + shf_ref[0]
    vx_o[0] = v.astype(BF16)
    logits = _nt_dot(wr_ref[...], v, HIGHEST)
    e = jnp.exp(logits - jnp.max(logits, axis=0, keepdims=True))
    aff_o[0] = e / jnp.sum(e, axis=0, keepdims=True)


def _merge(ya, yb, yc, gates, hx, lw, gm, shf, scf):
    b, t, d = hx.shape
    tm = _pick_tile(t, (512, 256))
    const = lambda shape: pl.BlockSpec(shape, lambda i, j: (0,) * len(shape))
    tok = lambda w: pl.BlockSpec((1, tm, w), lambda i, j: (i, j, 0))
    vec = pl.BlockSpec((1, 1, d), lambda i, j: (i, 0, 0))
    return pl.pallas_call(
        _merge_kernel,
        out_shape=[jax.ShapeDtypeStruct((b, t, d), F32), jax.ShapeDtypeStruct((b, t, d), BF16),
                   jax.ShapeDtypeStruct((b, N_EXPERTS, t), F32)],
        grid=(b, t // tm),
        in_specs=[tok(A_Q_DIM), tok(NA_DIM), tok(SSM_INNER), tok(N_BRANCH * d), tok(d),
                  const((A_Q_DIM, d)), const((NA_DIM, d)), const((SSM_INNER, d)), const((d, d)),
                  vec, const((1, d)), const((1, d)), vec, vec, const((N_EXPERTS, d))],
        out_specs=[tok(d), tok(d), pl.BlockSpec((1, N_EXPERTS, tm), lambda i, j: (i, 0, j))],
        compiler_params=_cparams(("parallel", "parallel")),
        name="merge_residual_router",
    )(ya, yb, yc, gates, hx, lw["w_a"], lw["w_b"], lw["w_c"], lw["w_out"], gm, lw["n_post"],
      lw["n_ffn_pre"], shf, scf, lw["w_rt"])


def _topk_kernel(cap, aff_ref, pos_o, gate_o, cs_o, eq_s, gt_s, ceq_s, csel_s):
    x = aff_ref[0]
    ne, t = x.shape
    bits = pltpu.bitcast(x, I32)
    theta = jnp.zeros((ne, 1), I32)
    for bit in range(30, -1, -1):
        cand = theta | (1 << bit)
        cnt = jnp.sum((bits >= cand).astype(I32), axis=-1, keepdims=True)
        theta = jnp.where(cnt >= cap, cand, theta)
    gt = bits > theta
    eq = bits == theta
    need = cap - jnp.sum(gt.astype(I32), axis=-1, keepdims=True)
    eq_s[...] = eq.astype(F32)
    gt_s[...] = gt.astype(F32)
    tb = ROUTE_TILE
    ii = lax.broadcasted_iota(I32, (tb, tb), 0)
    jj = lax.broadcasted_iota(I32, (tb, tb), 1)
    upper = (ii <= jj).astype(BF16)
    lane = lax.broadcasted_iota(I32, (ne, V7X_LANES), 1)
    needf = need.astype(F32)

    ceq_s[...] = jnp.zeros(ceq_s.shape, F32)
    csel_s[...] = jnp.zeros(csel_s.shape, F32)
    cs_o[0] = jnp.zeros((ne, V7X_LANES), I32)

    def body(k, carry):
        c_eq, c_sel = ceq_s[...], csel_s[...]
        sl = pl.ds(pl.multiple_of(k * tb, tb), tb)
        e_k = eq_s[:, sl]
        g_k = gt_s[:, sl]
        inc_eq = _dot(e_k.astype(BF16), upper)
        rank_eq = c_eq + inc_eq - e_k
        sel = g_k + e_k * (rank_eq < needf).astype(F32)
        inc_sel = _dot(sel.astype(BF16), upper)
        slot = c_sel + inc_sel - sel
        pos_o[0, :, sl] = jnp.where(sel > 0.5, slot, -1.0).astype(I32)
        gate_o[0, :, sl] = jnp.where(sel > 0.5, aff_ref[0, :, sl], 0.0)
        cs_o[0] = jnp.where(lane == k, c_sel.astype(I32), cs_o[0])
        ceq_s[...] = c_eq + inc_eq[:, tb - 1:tb]
        csel_s[...] = c_sel + inc_sel[:, tb - 1:tb]
        return carry

    nk = t // tb
    lax.fori_loop(0, nk, body, 0)
    cs_o[0] = jnp.where(lane == nk, csel_s[...].astype(I32), cs_o[0])


def _topk(aff_t, cap):
    b, ne, t = aff_t.shape
    assert t % ROUTE_TILE == 0 and t // ROUTE_TILE < V7X_LANES
    full = pl.BlockSpec((1, ne, t), lambda i: (i, 0, 0))
    return pl.pallas_call(
        functools.partial(_topk_kernel, cap),
        out_shape=[jax.ShapeDtypeStruct((b, ne, t), I32), jax.ShapeDtypeStruct((b, ne, t), F32),
                   jax.ShapeDtypeStruct((b, ne, V7X_LANES), I32)],
        grid=(b,),
        in_specs=[full],
        out_specs=[full, full, pl.BlockSpec((1, ne, V7X_LANES), lambda i: (i, 0, 0))],
        scratch_shapes=[pltpu.VMEM((ne, t), F32), pltpu.VMEM((ne, t), F32),
                        pltpu.VMEM((ne, 1), F32), pltpu.VMEM((ne, 1), F32)],
        compiler_params=_cparams(("parallel",)),
        name="expert_topk",
    )(aff_t)


def _cs_at(cs_ref, b, e, k):
    return cs_ref[(b * N_EXPERTS + e) * V7X_LANES + k]


_GATHER_SMALL = 64


_GATHER_TILES = 8


def _gather_kernel(cap, cs_ref, pos_ref, vx_ref, xs_o):
    b, e, g = pl.program_id(0), pl.program_id(1), pl.program_id(2)
    tt = ROUTE_TILE
    nt = vx_ref.shape[1] // tt
    narrow = min(2 * _GATHER_SMALL, cap)
    wide = min(_GATHER_SMALL + tt, cap)

    @pl.when(g == 0)
    def _():
        xs_o[...] = jnp.zeros(xs_o.shape, xs_o.dtype)

    def tile(i, carry):
        k = g * nt + i
        c0 = _cs_at(cs_ref, b, e, k)
        n_sel = _cs_at(cs_ref, b, e, k + 1) - c0
        toks = pl.ds(pl.multiple_of(i * tt, tt), tt)
        slot = pos_ref[0, 0, :, toks]

        def place(wn):
            w0 = jnp.minimum((c0 // _GATHER_SMALL) * _GATHER_SMALL, cap - wn)
            w0 = pl.multiple_of(w0, min(_GATHER_SMALL, cap))
            onehot = (lax.broadcasted_iota(I32, (wn, tt), 0) + w0 == slot).astype(BF16)
            rows = _dot(onehot, vx_ref[0, toks, :])
            win = xs_o[0, 0, pl.ds(w0, wn), :].astype(F32)
            xs_o[0, 0, pl.ds(w0, wn), :] = (win + rows).astype(xs_o.dtype)

        if narrow < wide:
            pl.when(jnp.logical_and(n_sel > 0, n_sel <= _GATHER_SMALL))(lambda: place(narrow))
            pl.when(n_sel > _GATHER_SMALL)(lambda: place(wide))
        else:
            pl.when(n_sel > 0)(lambda: place(wide))
        return carry

    lax.fori_loop(0, nt, tile, 0)


def _gather(cs_flat, pos_t, vx, cap):
    b, t, d = vx.shape
    ne = N_EXPERTS
    nt = min(_GATHER_TILES, t // ROUTE_TILE)
    tg = nt * ROUTE_TILE
    assert t % tg == 0 and (cap % _GATHER_SMALL == 0 or cap < _GATHER_SMALL)
    grid_spec = pltpu.PrefetchScalarGridSpec(
        num_scalar_prefetch=1,
        grid=(b, ne, t // tg),
        in_specs=[pl.BlockSpec((1, 1, 1, tg), lambda i, e, g, cs: (i, e, 0, g)),
                  pl.BlockSpec((1, tg, d), lambda i, e, g, cs: (i, g, 0))],
        out_specs=pl.BlockSpec((1, 1, cap, d), lambda i, e, g, cs: (i, e, 0, 0)))
    return pl.pallas_call(
        functools.partial(_gather_kernel, cap),
        out_shape=jax.ShapeDtypeStruct((b, ne, cap, d), BF16),
        grid_spec=grid_spec,
        compiler_params=_cparams(("parallel", "parallel", "arbitrary")),
        name="expert_gather",
    )(cs_flat, pos_t.reshape(b, ne, 1, t), vx)


_FFN_TILE_F = 512


def _ffn_kernel(x_ref, wg_ref, wu_ref, wd_ref, y_o, acc_s):
    j = pl.program_id(2)
    x = x_ref[0, 0]
    hid = _silu(_dot(x, wg_ref[0, 0].astype(BF16))) * _dot(x, wu_ref[0, 0].astype(BF16))
    part = _dot(hid.astype(BF16), wd_ref[0, 0].astype(BF16))

    @pl.when(j == 0)
    def _():
        acc_s[...] = part

    @pl.when(j > 0)
    def _():
        acc_s[...] += part

    @pl.when(j == pl.num_programs(2) - 1)
    def _():
        y_o[0, 0] = acc_s[...].astype(y_o.dtype)


def _expert_ffn(xs, layer, wg, wu, wd):
    b, ne, cap, d = xs.shape
    f = wg.shape[3]
    tf = _FFN_TILE_F
    tok = pl.BlockSpec((1, 1, cap, d), lambda e, i, j: (i, e, 0, 0))
    return pl.pallas_call(
        _ffn_kernel,
        out_shape=jax.ShapeDtypeStruct(xs.shape, BF16),
        grid=(ne, b, f // tf),
        in_specs=[tok,
                  pl.BlockSpec((1, 1, d, tf), lambda e, i, j: (layer, e, 0, j)),
                  pl.BlockSpec((1, 1, d, tf), lambda e, i, j: (layer, e, 0, j)),
                  pl.BlockSpec((1, 1, tf, d), lambda e, i, j: (layer, e, j, 0))],
        out_specs=tok,
        scratch_shapes=[pltpu.VMEM((cap, d), F32)],
        compiler_params=_cparams(("parallel", "parallel", "arbitrary")),
        name="expert_ffn",
    )(xs, wg, wu, wd)


_SLOT_ALIGN = 16


def _win_start(c0, cap, win):
    return jnp.minimum((c0 // _SLOT_ALIGN) * _SLOT_ALIGN, cap - win)


def _combine_kernel(cap, win, win_fast, cs_ref, *refs):
    ne = N_EXPERTS
    ys = refs[:ne]
    pos_ref, gate_ref, hx_ref, gf_ref, npost_ref, hx_o = refs[ne:]
    b, k = pl.program_id(0), pl.program_id(1)
    tt, d = hx_ref.shape[1], hx_ref.shape[2]
    pos, gate = pos_ref[0], gate_ref[0]
    starts, fits = [], None
    for e in range(ne):
        c0 = _cs_at(cs_ref, b, e, k)
        a = _win_start(c0, cap, win)
        starts.append(a)
        fit = _cs_at(cs_ref, b, e, k + 1) - a <= win_fast
        fits = fit if fits is None else jnp.logical_and(fits, fit)

    def run(width):
        lane = lax.broadcasted_iota(I32, (tt, width), 1)
        acc = jnp.zeros((tt, d), F32)
        for e in range(ne):
            local = pos[:, e:e + 1] - starts[e]
            onehot = jnp.where(lane == local, gate[:, e:e + 1], 0.0).astype(BF16)
            acc = acc + _dot(onehot, ys[e][0, 0, :width, :])
        hx_o[0] = hx_ref[0] + gf_ref[0] * _rms(acc, npost_ref[...])

    if win_fast < win:
        pl.when(fits)(lambda: run(win_fast))
        pl.when(jnp.logical_not(fits))(lambda: run(win))
    else:
        run(win)


def _combine(cs_flat, ys, pos_r, gate_r, hx, gf, n_post):
    b, t, d = hx.shape
    ne, cap = ys.shape[1], ys.shape[2]
    tt = ROUTE_TILE
    win = min(tt + _SLOT_ALIGN, cap)
    win_fast = min(tt, win)
    assert (cap - win) % _SLOT_ALIGN == 0

    def window(e):
        return pl.BlockSpec(
            (pl.Element(1), pl.Element(1), pl.Element(win), pl.Element(d)),
            lambda i, k, cs: (i, e, pl.multiple_of(_win_start(cs[(i * ne + e) * V7X_LANES + k], cap, win), _SLOT_ALIGN), 0))

    tok = lambda w: pl.BlockSpec((1, tt, w), lambda i, k, cs: (i, k, 0))
    grid_spec = pltpu.PrefetchScalarGridSpec(
        num_scalar_prefetch=1,
        grid=(b, t // tt),
        in_specs=[window(e) for e in range(ne)] + [
            tok(ne), tok(ne), tok(d),
            pl.BlockSpec((1, 1, d), lambda i, k, cs: (i, 0, 0)),
            pl.BlockSpec((1, d), lambda i, k, cs: (0, 0))],
        out_specs=tok(d))
    return pl.pallas_call(
        functools.partial(_combine_kernel, cap, win, win_fast),
        out_shape=jax.ShapeDtypeStruct((b, t, d), F32),
        grid_spec=grid_spec,
        compiler_params=_cparams(("parallel", "parallel")),
        name="expert_combine",
    )(cs_flat, *([ys] * ne), pos_r, gate_r, hx, gf, n_post)


def _moe(vx, aff_t, hx, gf, lw):
    b, t, d = hx.shape
    cap = EC_CAPACITY * t // N_EXPERTS
    pos_t, gate_t, cs = _topk(aff_t, cap)
    cs_flat = cs.reshape(-1)
    xs = _gather(cs_flat, pos_t, vx, cap)
    ys = _expert_ffn(xs, lw["layer"], lw["w_gate"], lw["w_up"], lw["w_down"])
    return _combine(cs_flat, ys, pos_t.transpose(0, 2, 1), gate_t.transpose(0, 2, 1), hx, gf,
                    lw["n_ffn_post"])


def _rope_tables(s):
    t = jnp.arange(s, dtype=I32)
    rows = (t // GRID_W).astype(F32)
    cols = (t % GRID_W).astype(F32)
    n_freq = HEAD_DIM // 4
    inv = ROPE_BASE ** (-jnp.arange(n_freq, dtype=F32) / n_freq)
    ar = rows[:, None] * inv[None, :]
    ac = cols[:, None] * inv[None, :]
    ang = jnp.concatenate([ar, ar, ac, ac] * (V7X_LANES // HEAD_DIM), axis=-1)
    return jnp.cos(ang), jnp.sin(ang)


def _layer_weights(i, p):
    d = p["w_in"].shape[1]
    w_in = p["w_in"][i]
    dt_pad = jnp.zeros((d, V7X_LANES - N_DIRS * SSM_HEADS), F32)
    hm = np.kron(np.eye(A_HEADS, dtype=np.float32), np.full((HEAD_DIM, HEAD_DIM), 1.0 / HEAD_DIM, np.float32))
    lane_pad = lambda v: jnp.pad(v.reshape(1, -1), ((0, 0), (0, V7X_LANES - v.size)))
    a = -jnp.exp(p["ssm_a_log"][i].astype(F32)).reshape(-1)
    return {
        "w_main": jnp.concatenate([w_in[:, :_DT_COL], w_in[:, _DT_COL + N_DIRS * SSM_HEADS:]], axis=1).astype(BF16),
        "w_dt": jnp.concatenate([w_in[:, _DT_COL:_DT_COL + N_DIRS * SSM_HEADS], dt_pad], axis=1).astype(BF16),
        "qn": jnp.tile(p["q_norm"][i], A_HEADS).reshape(1, -1),
        "kn": jnp.tile(p["k_norm"][i], A_KV_HEADS).reshape(1, -1),
        "hm": jnp.asarray(hm, BF16),
        "n_pre": p["norm_mix_pre"][i].reshape(1, d),
        "n_post": p["norm_mix_post"][i].reshape(1, d),
        "n_ffn_pre": p["norm_ffn_pre"][i].reshape(1, d),
        "n_ffn_post": p["norm_ffn_post"][i].reshape(1, d),
        "bias_tab": _na_bias_table(p["na_rel_bias"][i]),
        "conv_w": jnp.pad(p["conv_w"][i], ((0, 8 - SSM_CONV), (0, 0))),
        "conv_b": p["conv_b"][i].reshape(1, -1),
        "dt_b": lane_pad(p["ssm_dt_bias"][i]),
        "dt_bt": p["ssm_dt_bias"][i].reshape(-1, 1),
        "a_row": lane_pad(a),
        "a_col": a.reshape(-1, 1),
        "d_skip": jnp.repeat(p["ssm_d"][i, 0] + p["ssm_d"][i, 1], SSM_HEAD_DIM).reshape(1, -1),
        "ssm_norm": p["ssm_norm"][i].reshape(1, -1),
        "w_a": p["w_branch_a"][i].astype(BF16),
        "w_b": p["w_branch_b"][i].astype(BF16),
        "w_c": p["w_branch_c"][i].astype(BF16),
        "w_out": p["w_out"][i].astype(BF16),
        "w_rt": p["w_router"][i].T,
        "layer": i,
        "w_gate": p["w_exp_gate"],
        "w_up": p["w_exp_up"],
        "w_down": p["w_exp_down"],
    }


def _ssd_branch(px, pc, lw, with_ctx_out):
    zx, xbcx, dtx = px
    zc, xbcc, dtc = pc
    b = zx.shape[0]
    nd = N_DIRS * SSM_HEADS
    dttx = dtx[:, :, :nd].transpose(0, 2, 1)
    dttc = dtc[:, :, :nd].transpose(0, 2, 1)
    h0 = jnp.zeros((b, SSM_HEADS, SSM_HEAD_DIM, SSM_STATE), F32)
    y0c, hc0 = _ssd_pass(0, xbcc, dtc, dttc, lw, h0)
    yc, hc1 = _ssd_pass(1, xbcc, dtc, dttc, lw, h0, fin=(y0c, zc))
    y0x, _ = _ssd_pass(0, xbcx, dtx, dttx, lw, hc0)
    yx, _ = _ssd_pass(1, xbcx, dtx, dttx, lw, hc1, fin=(y0x, zx))
    return yx, (yc if with_ctx_out else None)


def _layer(i, hx, hc, mod_x, mod_c, p, rope_tabs, last):
    lw = _layer_weights(i, p)
    d = hx.shape[2]
    sh_m, sc_m, g_m, sh_f, sc_f, g_f = [mod_x[:, None, j * d:(j + 1) * d] for j in range(N_ADA)]
    csh_m, csc_m, cg_m, csh_f, csc_f, cg_f = [mod_c[:, None, j * d:(j + 1) * d] for j in range(N_ADA)]

    qa_c, ka_c, va_c, qb_c, kb_c, vb_c, z_c, xbc_c, dt_c, gt_c = _proj(hc, lw["n_pre"], csh_m, csc_m, lw, None)
    qa, ka, va, qb, kb, vb, z, xbc, dt, gt = _proj(hx, lw["n_pre"], sh_m, sc_m, lw, rope_tabs)

    ya = _flash(qa, jnp.concatenate([ka, ka_c], axis=1), jnp.concatenate([va, va_c], axis=1),
                A_KV_HEADS, A_HEADS // A_KV_HEADS, True)
    yb = _na(qb, kb, vb, kb_c, vb_c, lw["bias_tab"])
    yc, yc_c = _ssd_branch((z, xbc, dt), (z_c, xbc_c, dt_c), lw, not last)
    hx, vx, aff = _merge(ya, yb, yc, gt, hx, lw, g_m, sh_f, sc_f)
    hx = _moe(vx, aff, hx, g_f, lw)
    if last:
        return hx, hc
    ya_c = _flash(qa_c, ka_c, va_c, A_KV_HEADS, A_HEADS // A_KV_HEADS, True)
    yb_c = _flash(qb_c, kb_c, vb_c, NA_HEADS, 1, False)
    hc, vc, aff_c = _merge(ya_c, yb_c, yc_c, gt_c, hc, lw, cg_m, csh_f, csc_f)
    hc = _moe(vc, aff_c, hc, cg_f, lw)
    return hx, hc


def kernel(x, c, ctx, c_ctx, w_ada, b_ada, norm_mix_pre, norm_mix_post, norm_ffn_pre, norm_ffn_post, w_in, q_norm, k_norm, na_rel_bias, conv_w, conv_b, ssm_a_log, ssm_dt_bias, ssm_d, ssm_norm, w_branch_a, w_branch_b, w_branch_c, w_out, w_router, w_exp_gate, w_exp_up, w_exp_down):
    p = dict(norm_mix_pre=norm_mix_pre, norm_mix_post=norm_mix_post, norm_ffn_pre=norm_ffn_pre,
             norm_ffn_post=norm_ffn_post, w_in=w_in, q_norm=q_norm, k_norm=k_norm, na_rel_bias=na_rel_bias,
             conv_w=conv_w, conv_b=conv_b, ssm_a_log=ssm_a_log, ssm_dt_bias=ssm_dt_bias, ssm_d=ssm_d,
             ssm_norm=ssm_norm, w_branch_a=w_branch_a, w_branch_b=w_branch_b, w_branch_c=w_branch_c,
             w_out=w_out, w_router=w_router, w_exp_gate=w_exp_gate, w_exp_up=w_exp_up, w_exp_down=w_exp_down)
    b, s, d = x.shape
    depth = w_ada.shape[0]
    assert b + 1 <= 8
    cc = jnp.concatenate([c, c_ctx[None, :], jnp.zeros((8 - b - 1, d), F32)], axis=0)
    mod = _ada_mod(cc, w_ada, b_ada)
    rope_tabs = _rope_tables(s)
    hx, hc = x, ctx
    for i in range(depth):
        mod_x = mod[i, :b]
        mod_c = jnp.broadcast_to(mod[i, b:b + 1], (b, mod.shape[2]))
        hx, hc = _layer(i, hx, hc, mod_x, mod_c, p, rope_tabs, i == depth - 1)
    return hx
```

```python
import functools

import jax
import jax.numpy as jnp
import numpy as np
from jax import lax
from jax.experimental import pallas as pl
from jax.experimental.pallas import tpu as pltpu

F32, BF16, I32 = jnp.float32, jnp.bfloat16, jnp.int32
HIGHEST = lax.Precision.HIGHEST

GRID_W = 64
HEAD_DIM = 64
ROPE_BASE = 10000.0
NORM_EPS = 1e-6
N_ADA = 6
A_HEADS, A_KV_HEADS = 8, 2
NA_HEADS, NA_KH, NA_KW = 8, 8, 16
SSM_HEADS, SSM_HEAD_DIM, SSM_INNER = 8, 64, 512
SSM_GROUPS, SSM_STATE, SSM_CONV, N_DIRS = 2, 64, 5, 2
XBC_DIM = SSM_INNER + 2 * SSM_GROUPS * SSM_STATE
A_Q_DIM, A_KV_DIM, NA_DIM = A_HEADS * HEAD_DIM, A_KV_HEADS * HEAD_DIM, NA_HEADS * HEAD_DIM
N_EXPERTS, EC_CAPACITY = 16, 2
N_BRANCH = 3
ATT_SCALE = HEAD_DIM ** -0.5
LOG2_E = 1.4426950408889634

V7X_LANES = 128
V7X_VMEM_LIMIT_BYTES = 56 * 1024 * 1024
NEG_BIG = -1e30

SSD_CHUNK = 128
ROUTE_TILE = 256
SLOT_BLOCK = 256


def _cparams(sem):
    return pltpu.CompilerParams(dimension_semantics=sem, vmem_limit_bytes=V7X_VMEM_LIMIT_BYTES)


def _nt_dot(a, b, precision=None):
    return lax.dot_general(a, b, (((1,), (1,)), ((), ())), preferred_element_type=F32,
                           precision=precision)


def _dot(a, b, precision=None):
    return jnp.dot(a, b, preferred_element_type=F32, precision=precision)


def _silu(x):
    return x * jax.nn.sigmoid(x)


def _rms(x, gain):
    return x * lax.rsqrt(jnp.mean(x * x, axis=-1, keepdims=True) + NORM_EPS) * gain


def _ada_kernel(c_ref, w_ref, b_ref, o_ref):
    o_ref[0] = _dot(_silu(c_ref[...]), w_ref[0], HIGHEST) + b_ref[0]


def _ada_mod(cc, w_ada, b_ada):
    nl, d, n = w_ada.shape
    tn = 512
    return pl.pallas_call(
        _ada_kernel,
        out_shape=jax.ShapeDtypeStruct((nl, 8, n), F32),
        grid=(nl, n // tn),
        in_specs=[pl.BlockSpec((8, d), lambda l, j: (0, 0)),
                  pl.BlockSpec((1, d, tn), lambda l, j: (l, 0, j)),
                  pl.BlockSpec((1, 1, tn), lambda l, j: (l, 0, j))],
        out_specs=pl.BlockSpec((1, 8, tn), lambda l, j: (l, 0, j)),
        compiler_params=_cparams(("parallel", "parallel")),
        name="ada_mod",
    )(cc, w_ada, b_ada.reshape(nl, 1, n))


_OFF_QA, _OFF_KA, _OFF_VA, _OFF_QB, _OFF_KB, _OFF_VB, _OFF_Z, _OFF_XBC, _OFF_G, _OFF_END = (
    0, 512, 640, 768, 1280, 1792, 2304, 2816, 3584, 6656)
_DT_COL = 3584


def _head_rms(x, hm, gain):
    ms = _dot((x * x).astype(BF16), hm)
    return x * lax.rsqrt(ms + NORM_EPS) * gain


def _rope(x, cos, sin):
    w = x.shape[-1]
    lane = lax.broadcasted_iota(I32, x.shape, 1)
    first = (lane & 16) == 0
    rot = jnp.where(first, -pltpu.roll(x, w - 16, 1), pltpu.roll(x, 16, 1))
    return x * cos + rot * sin


def _proj_kernel(rope, h_ref, g_ref, sh_ref, sc_ref, w_ref, wdt_ref, qn_ref, kn_ref, hm_ref, *rest):
    if rope:
        cos_ref, sin_ref = rest[:2]
        rest = rest[2:]
    qa_o, ka_o, va_o, qb_o, kb_o, vb_o, z_o, xbc_o, dt_o, gt_o = rest
    u = _rms(h_ref[0], g_ref[...]) * (1.0 + sc_ref[0]) + sh_ref[0]
    ub = u.astype(BF16)

    def mm(lo, hi):
        return _dot(ub, w_ref[:, lo:hi])

    qa = _head_rms(mm(_OFF_QA, _OFF_KA), hm_ref[...], qn_ref[...])
    ka = _head_rms(mm(_OFF_KA, _OFF_VA), hm_ref[:A_KV_DIM, :A_KV_DIM], kn_ref[...])
    if rope:
        cs, sn = cos_ref[...], sin_ref[...]
        qa = _rope(qa, jnp.concatenate([cs] * 4, axis=1), jnp.concatenate([sn] * 4, axis=1))
        ka = _rope(ka, cs, sn)
    qa_o[0] = (qa * (ATT_SCALE * LOG2_E)).astype(BF16)
    ka_o[0] = ka.astype(BF16)
    va_o[0] = mm(_OFF_VA, _OFF_QB).astype(BF16)
    qb_o[0] = (mm(_OFF_QB, _OFF_KB) * ATT_SCALE).astype(BF16)
    kb_o[0] = mm(_OFF_KB, _OFF_VB).astype(BF16)
    vb_o[0] = mm(_OFF_VB, _OFF_Z).astype(BF16)
    z_o[0] = mm(_OFF_Z, _OFF_XBC)
    xbc_o[0] = mm(_OFF_XBC, _OFF_G)
    dt_o[0] = _dot(ub, wdt_ref[...])
    for i in range((_OFF_END - _OFF_G) // 512):
        gt_o[0, :, i * 512:(i + 1) * 512] = mm(_OFF_G + i * 512, _OFF_G + (i + 1) * 512)


def _proj(h, gain, shift, scale, lw, rope_tabs):
    b, t, d = h.shape
    tm = 256
    rope = rope_tabs is not None
    const = lambda shape: pl.BlockSpec(shape, lambda i, j: (0,) * len(shape))
    tok = lambda w: pl.BlockSpec((1, tm, w), lambda i, j: (i, j, 0))
    vec = pl.BlockSpec((1, 1, d), lambda i, j: (i, 0, 0))
    in_specs = [tok(d), const((1, d)), vec, vec, const((d, _OFF_END)), const((d, V7X_LANES)),
                const((1, A_Q_DIM)), const((1, A_KV_DIM)), const((A_Q_DIM, A_Q_DIM))]
    args = [h, gain, shift, scale, lw["w_main"], lw["w_dt"], lw["qn"], lw["kn"], lw["hm"]]
    if rope:
        tab = pl.BlockSpec((tm, V7X_LANES), lambda i, j: (j, 0))
        in_specs += [tab, tab]
        args += list(rope_tabs)
    widths = [(A_Q_DIM, BF16), (A_KV_DIM, BF16), (A_KV_DIM, BF16), (NA_DIM, BF16), (NA_DIM, BF16),
              (NA_DIM, BF16), (SSM_INNER, F32), (XBC_DIM, F32), (V7X_LANES, F32),
              (_OFF_END - _OFF_G, F32)]
    return pl.pallas_call(
        functools.partial(_proj_kernel, rope),
        out_shape=[jax.ShapeDtypeStruct((b, t, w), dt) for w, dt in widths],
        grid=(b, t // tm),
        in_specs=in_specs,
        out_specs=[tok(w) for w, _ in widths],
        compiler_params=_cparams(("parallel", "parallel")),
        name="norm_mod_proj",
    )(*args)


def _flash_kernel(nkv, grp, base2, q_ref, kt_ref, v_ref, o_ref, q_s, m_s, acc_s):
    j = pl.program_id(2)
    tq = q_ref.shape[1]
    tk = kt_ref.shape[2]
    hd = HEAD_DIM
    ex = jnp.exp2 if base2 else jnp.exp

    @pl.when(j == 0)
    def _():
        qb = q_ref[0]
        for g in range(nkv):
            parts = [qb[:, (g * grp + i) * hd:(g * grp + i + 1) * hd] for i in range(grp)]
            q_s[g] = parts[0] if grp == 1 else jnp.concatenate(parts, axis=0)
        m_s[...] = jnp.full(m_s.shape, -jnp.inf, F32)
        acc_s[...] = jnp.zeros(acc_s.shape, F32)

    for g in range(nkv):
        s = _dot(q_s[g], kt_ref[0, g * hd:(g + 1) * hd, :])
        m_prev = m_s[g]
        m_new = jnp.maximum(m_prev, jnp.max(s, axis=-1, keepdims=True))
        alpha = ex(m_prev - m_new)
        m_s[g] = m_new
        p = jnp.concatenate(
            [ex(s[:, c * V7X_LANES:(c + 1) * V7X_LANES] - m_new).astype(BF16) for c in range(tk // V7X_LANES)],
            axis=1)
        acc_s[g] = alpha * acc_s[g] + _dot(p, v_ref[0, :, g * V7X_LANES:(g + 1) * V7X_LANES])

    @pl.when(j == pl.num_programs(2) - 1)
    def _():
        for g in range(nkv):
            acc = acc_s[g]
            o = acc / pltpu.roll(acc, hd, 1)
            for i in range(grp):
                c0 = (g * grp + i) * hd
                o_ref[0, :, c0:c0 + hd] = o[i * tq:(i + 1) * tq, :hd].astype(o_ref.dtype)


def _pick_tile(n, cands):
    for c in cands:
        if n % c == 0:
            return c
    raise ValueError(f"no tile for {n}")


def _flash(q, k, v, nkv, grp, base2):
    b, s, qd = q.shape
    sk = k.shape[1]
    kt = k.transpose(0, 2, 1)
    ones = jnp.ones((b, sk, nkv, V7X_LANES - HEAD_DIM), v.dtype)
    vaug = jnp.concatenate([v.reshape(b, sk, nkv, HEAD_DIM), ones], axis=-1).reshape(b, sk, nkv * V7X_LANES)
    tq = _pick_tile(s, (512, 256, 128))
    tk = _pick_tile(sk, (3328, 1280, 1024, 512, 256, 128))
    rows = grp * tq
    return pl.pallas_call(
        functools.partial(_flash_kernel, nkv, grp, base2),
        out_shape=jax.ShapeDtypeStruct((b, s, qd), BF16),
        grid=(b, s // tq, sk // tk),
        in_specs=[pl.BlockSpec((1, tq, qd), lambda i, m, j: (i, m, 0)),
                  pl.BlockSpec((1, nkv * HEAD_DIM, tk), lambda i, m, j: (i, 0, j)),
                  pl.BlockSpec((1, tk, nkv * V7X_LANES), lambda i, m, j: (i, j, 0))],
        out_specs=pl.BlockSpec((1, tq, qd), lambda i, m, j: (i, m, 0)),
        scratch_shapes=[pltpu.VMEM((nkv, rows, HEAD_DIM), BF16),
                        pltpu.VMEM((nkv, rows, V7X_LANES), F32),
                        pltpu.VMEM((nkv, rows, V7X_LANES), F32)],
        compiler_params=_cparams(("parallel", "parallel", "arbitrary")),
        name=f"flash_attn_{nkv}x{grp}",
    )(q, kt, vaug)


_NA_ROWS_PER_STEP = 8
_NA_BLOCK = _NA_ROWS_PER_STEP * GRID_W
_NA_BAND = NA_KH * GRID_W


def _na_kernel(rows, q_ref, kp_ref, kc_ref, kn_ref, vp_ref, vc_ref, vn_ref, kx_ref, vx_ref, bias_ref,
               o_ref, kcat_s, vcat_s):
    g = pl.program_id(1)
    nb = _NA_BLOCK
    for i, (kr, vr) in enumerate(((kp_ref, vp_ref), (kc_ref, vc_ref), (kn_ref, vn_ref))):
        kcat_s[i * nb:(i + 1) * nb, :] = kr[0]
        vcat_s[i * nb:(i + 1) * nb, :] = vr[0]
    n_ctx = kx_ref.shape[1]
    lanes = V7X_LANES
    head0 = lax.broadcasted_iota(I32, (GRID_W, lanes), 1) < HEAD_DIM
    ones_loc = jnp.ones((_NA_BAND, lanes), BF16)
    ones_ctx = jnp.ones((n_ctx, lanes), BF16)

    def body(j, carry):
        r = g * _NA_ROWS_PER_STEP + j
        r0 = jnp.clip(r - NA_KH // 2, 0, rows - NA_KH)
        off = pl.multiple_of((r0 - g * _NA_ROWS_PER_STEP + _NA_ROWS_PER_STEP) * GRID_W, GRID_W)
        case = r - r0
        qrows = pl.ds(pl.multiple_of(j * GRID_W, GRID_W), GRID_W)
        for pr in range(NA_HEADS // 2):
            ls = slice(pr * lanes, (pr + 1) * lanes)
            qp = q_ref[0, qrows, ls]
            zero = jnp.zeros_like(qp)
            q2 = jnp.concatenate([jnp.where(head0, qp, zero), jnp.where(head0, zero, qp)], axis=0)
            kband = kcat_s[pl.ds(off, _NA_BAND), ls]
            vband = vcat_s[pl.ds(off, _NA_BAND), ls]
            s_loc = _nt_dot(q2, kband) + bias_ref[case * (NA_HEADS // 2) + pr]
            s_ctx = _nt_dot(q2, kx_ref[0, :, ls])
            m = jnp.maximum(jnp.max(s_loc, axis=-1, keepdims=True), jnp.max(s_ctx, axis=-1, keepdims=True))
            p_loc = jnp.exp(s_loc - m).astype(BF16)
            p_ctx = jnp.exp(s_ctx - m).astype(BF16)
            pvl = (_dot(p_loc, jnp.concatenate([vband, ones_loc], axis=1))
                   + _dot(p_ctx, jnp.concatenate([vx_ref[0, :, ls], ones_ctx], axis=1)))
            o2 = pvl[:, :lanes] / pvl[:, lanes:]
            o_ref[0, qrows, ls] = jnp.where(head0, o2[:GRID_W], o2[GRID_W:]).astype(o_ref.dtype)
        return carry

    lax.fori_loop(0, _NA_ROWS_PER_STEP, body, 0, unroll=2)


def _na_bias_table(rel_bias):
    nh, nr, nc = rel_bias.shape
    col = np.arange(GRID_W)
    c0 = np.clip(col - NA_KW // 2, 0, GRID_W - NA_KW)
    valid = (col[None, :] >= c0[:, None]) & (col[None, :] < c0[:, None] + NA_KW)
    w2 = 2 * GRID_W
    lo = GRID_W - NA_KW
    padded = jnp.pad(rel_bias.astype(F32), ((0, 0), (0, 0), (lo, w2 - lo - nc)))
    flat = jnp.broadcast_to(padded[:, :, None, :], (nh, nr, GRID_W, w2)).reshape(nh, nr, GRID_W * w2)
    skew = flat[:, :, :GRID_W * (w2 - 1)].reshape(nh, nr, GRID_W, w2 - 1)
    toe = skew[:, :, :, GRID_W - 1:]
    tab = jnp.stack([toe[:, NA_KH - 1 - case:2 * NA_KH - 1 - case] for case in range(NA_KH)])
    tab = jnp.where(valid[None, None, None], tab, NEG_BIG)
    tab = tab.transpose(0, 1, 3, 2, 4)
    return tab.reshape(NA_KH * NA_HEADS // 2, 2 * GRID_W, NA_KH * GRID_W)


def _na(q, k, v, kx, vx, bias_tab):
    b, s, d = q.shape
    l = kx.shape[1]
    rows = s // GRID_W
    assert rows % _NA_ROWS_PER_STEP == 0 and rows >= NA_KH
    ng = rows // _NA_ROWS_PER_STEP
    nb = _NA_BLOCK
    cur = pl.BlockSpec((1, nb, d), lambda i, g: (i, g, 0))
    prv = pl.BlockSpec((1, nb, d), lambda i, g: (i, jnp.maximum(g - 1, 0), 0))
    nxt = pl.BlockSpec((1, nb, d), lambda i, g: (i, jnp.minimum(g + 1, ng - 1), 0))
    ctx = pl.BlockSpec((1, l, d), lambda i, g: (i, 0, 0))
    return pl.pallas_call(
        functools.partial(_na_kernel, rows),
        out_shape=jax.ShapeDtypeStruct((b, s, d), BF16),
        grid=(b, ng),
        in_specs=[cur, prv, cur, nxt, prv, cur, nxt, ctx, ctx,
                  pl.BlockSpec(bias_tab.shape, lambda i, g: (0, 0, 0))],
        out_specs=cur,
        scratch_shapes=[pltpu.VMEM((3 * nb, d), BF16), pltpu.VMEM((3 * nb, d), BF16)],
        compiler_params=_cparams(("parallel", "parallel")),
        name="neighbourhood_attn",
    )(q, k, k, k, v, v, v, kx, vx, bias_tab)


def _softplus(x):
    return jnp.maximum(x, 0.0) + jnp.log1p(jnp.exp(-jnp.abs(x)))


def _ssd_kernel(d, final, xc_ref, xp_ref, xn_ref, dt_ref, dtt_ref, cw_ref, cb_ref, dtb_ref, dtbt_ref,
                a_ref, at_ref, h0_ref, *rest):
    if final:
        y0_ref, z_ref, dsk_ref, ng_ref, y_o, h_o, h_s, xcat_s = rest
    else:
        y0_ref = z_ref = dsk_ref = ng_ref = None
        y_o, h_o, h_s, xcat_s = rest
    t = pl.program_id(0)
    nc = pl.num_programs(0)
    reverse = d == 1
    c = nc - 1 - t if reverse else t
    q = xc_ref.shape[1]

    @pl.when(t == 0)
    def _():
        h_s[...] = h0_ref[...]

    pm = (c > 0).astype(F32)
    nm = (c < nc - 1).astype(F32)
    ii = lax.broadcasted_iota(I32, (q, q), 0)
    jj = lax.broadcasted_iota(I32, (q, q), 1)
    keep = (jj >= ii) if reverse else (jj <= ii)
    tri = keep.astype(F32)
    chan = lax.broadcasted_iota(I32, (V7X_LANES, SSM_INNER), 1)
    dt_col = lax.broadcasted_iota(I32, (V7X_LANES, SSM_INNER), 0)
    expand = (dt_col == d * SSM_HEADS + lax.shift_right_logical(chan, 6)).astype(F32)
    for bi in range(xc_ref.shape[0]):
        _ssd_chunk(d, bi, pm, nm, keep, tri, expand, xc_ref, xp_ref, xn_ref, dt_ref, dtt_ref, cw_ref, cb_ref,
                   dtb_ref, dtbt_ref, a_ref, at_ref, y0_ref, z_ref, dsk_ref, ng_ref, y_o, h_s, xcat_s)

    @pl.when(t == nc - 1)
    def _():
        h_o[...] = h_s[...]


def _ssd_chunk(d, bi, pm, nm, keep, tri, expand, xc_ref, xp_ref, xn_ref, dt_ref, dtt_ref, cw_ref, cb_ref,
               dtb_ref, dtbt_ref, a_ref, at_ref, y0_ref, z_ref, dsk_ref, ng_ref, y_o, h_s, xcat_s):
    final = y0_ref is not None
    reverse = d == 1
    q = xc_ref.shape[1]
    hp = SSM_HEAD_DIM
    ns = SSM_STATE

    xcat_s[bi, 0:8, :] = xp_ref[bi] * pm
    xcat_s[bi, 8:8 + q, :] = xc_ref[bi]
    xcat_s[bi, 8 + q:16 + q, :] = xn_ref[bi] * nm
    acc = jnp.zeros((q, XBC_DIM), F32)
    for k in range(SSM_CONV):
        acc = acc + xcat_s[bi, 8 - SSM_CONV // 2 + k:8 - SSM_CONV // 2 + k + q, :] * cw_ref[k:k + 1, :]
    u = _silu(acc + cb_ref[...])
    xs = u[:, :SSM_INNER]
    bm = u[:, SSM_INNER:SSM_INNER + SSM_GROUPS * ns]
    cm = u[:, SSM_INNER + SSM_GROUPS * ns:]

    dt = _softplus(dt_ref[bi] + dtb_ref[...])
    dtt = _softplus(dtt_ref[bi] + dtbt_ref[...])
    acs = _dot(tri, dt * a_ref[...], HIGHEST)
    acst = _nt_dot(dtt * at_ref[...], tri, HIGHEST)
    last = 0 if reverse else q - 1

    xdt_all = xs * _dot(dt, expand, HIGHEST)
    xdt_t = [xdt_all[:, p * V7X_LANES:(p + 1) * V7X_LANES].T for p in range(SSM_INNER // V7X_LANES)]
    xdt_b = xdt_all.astype(BF16)
    gmats = []
    for gi in range(SSM_GROUPS):
        gmats.append(_nt_dot(cm[:, gi * ns:(gi + 1) * ns].astype(BF16),
                             bm[:, gi * ns:(gi + 1) * ns].astype(BF16)))
    ys = []
    for h in range(SSM_HEADS):
        col = d * SSM_HEADS + h
        gi = h // (SSM_HEADS // SSM_GROUPS)
        a_col = acs[:, col:col + 1]
        a_row = acst[col:col + 1, :]
        tot = a_col[last:last + 1, :]
        lmat = jnp.exp(jnp.where(keep, a_col - a_row, NEG_BIG))
        y_diag = _dot((gmats[gi] * lmat).astype(BF16), xdt_b[:, h * hp:(h + 1) * hp])
        hst = h_s[bi, h]
        cg = cm[:, gi * ns:(gi + 1) * ns]
        bg = bm[:, gi * ns:(gi + 1) * ns]
        y_off = _nt_dot((cg * jnp.exp(a_col)).astype(BF16), hst.astype(BF16))
        ys.append(y_diag + y_off)
        bdec = (bg * jnp.exp(tot - a_col)).astype(BF16)
        xt = xdt_t[h // 2][(h % 2) * hp:(h % 2 + 1) * hp, :]
        h_s[bi, h] = jnp.exp(tot) * hst + _dot(xt.astype(BF16), bdec)
    y = jnp.concatenate(ys, axis=1)

    if final:
        yt = y0_ref[bi] + y + dsk_ref[...] * xs
        y_o[bi] = _rms(yt * _silu(z_ref[bi]), ng_ref[...]).astype(y_o.dtype)
    else:
        y_o[bi] = y


def _ssd_pass(d, xbc, dt, dtt, lw, h0, fin=None):
    b, l, _ = xbc.shape
    q = SSD_CHUNK
    nc = l // q
    final = fin is not None
    cidx = (lambda t: nc - 1 - t) if d == 1 else (lambda t: t)
    r8 = q // 8
    nb8 = l // 8
    const = lambda shape: pl.BlockSpec(shape, lambda t: (0,) * len(shape))
    tok = lambda w: pl.BlockSpec((b, q, w), lambda t: (0, cidx(t), 0))
    hspec = pl.BlockSpec((b, SSM_HEADS, SSM_HEAD_DIM, SSM_STATE), lambda t: (0, 0, 0, 0))
    in_specs = [
        tok(XBC_DIM),
        pl.BlockSpec((b, 8, XBC_DIM), lambda t: (0, jnp.maximum(cidx(t) * r8 - 1, 0), 0)),
        pl.BlockSpec((b, 8, XBC_DIM), lambda t: (0, jnp.minimum((cidx(t) + 1) * r8, nb8 - 1), 0)),
        tok(V7X_LANES),
        pl.BlockSpec((b, 16, q), lambda t: (0, 0, cidx(t))),
        const((8, XBC_DIM)), const((1, XBC_DIM)), const((1, V7X_LANES)), const((16, 1)),
        const((1, V7X_LANES)), const((16, 1)), hspec]
    args = [xbc, xbc, xbc, dt, dtt, lw["conv_w"], lw["conv_b"], lw["dt_b"], lw["dt_bt"],
            lw["a_row"], lw["a_col"], h0]
    if final:
        in_specs += [tok(SSM_INNER), tok(SSM_INNER), const((1, SSM_INNER)), const((1, SSM_INNER))]
        args += [fin[0], fin[1], lw["d_skip"], lw["ssm_norm"]]
    return pl.pallas_call(
        functools.partial(_ssd_kernel, d, final),
        out_shape=[jax.ShapeDtypeStruct((b, l, SSM_INNER), BF16 if final else F32),
                   jax.ShapeDtypeStruct(h0.shape, F32)],
        grid=(nc,),
        in_specs=in_specs,
        out_specs=[tok(SSM_INNER), hspec],
        scratch_shapes=[pltpu.VMEM((b, SSM_HEADS, SSM_HEAD_DIM, SSM_STATE), F32),
                        pltpu.VMEM((b, q + 16, XBC_DIM), F32)],
        compiler_params=_cparams(("arbitrary",)),
        name=f"ssd_scan_d{d}{'_final' if final else ''}",
    )(*args)


def _merge_kernel(ya_ref, yb_ref, yc_ref, gt_ref, hx_ref, wa_ref, wb_ref, wc_ref, wo_ref, gm_ref,
                  npost_ref, npre_ref, shf_ref, scf_ref, wr_ref, hx_o, vx_o, aff_o):
    d = hx_ref.shape[2]
    gt = gt_ref[0]
    m = (jax.nn.sigmoid(gt[:, :d]) * _dot(ya_ref[0], wa_ref[...])
         + jax.nn.sigmoid(gt[:, d:2 * d]) * _dot(yb_ref[0], wb_ref[...])
         + jax.nn.sigmoid(gt[:, 2 * d:]) * _dot(yc_ref[0], wc_ref[...]))
    out = _dot(m.astype(BF16), wo_ref[...])
    hx = hx_ref[0] + gm_ref[0] * _rms(out, npost_ref[...])
    hx_o[0] = hx
    v = _rms(hx, npre_ref[...]) * (1.0 + scf_ref[0]) + shf_ref[0]
    vx_o[0] = v.astype(BF16)
    logits = _nt_dot(wr_ref[...], v, HIGHEST)
    e = jnp.exp(logits - jnp.max(logits, axis=0, keepdims=True))
    aff_o[0] = e / jnp.sum(e, axis=0, keepdims=True)


def _merge(ya, yb, yc, gates, hx, lw, gm, shf, scf):
    b, t, d = hx.shape
    tm = _pick_tile(t, (512, 256))
    const = lambda shape: pl.BlockSpec(shape, lambda i, j: (0,) * len(shape))
    tok = lambda w: pl.BlockSpec((1, tm, w), lambda i, j: (i, j, 0))
    vec = pl.BlockSpec((1, 1, d), lambda i, j: (i, 0, 0))
    return pl.pallas_call(
        _merge_kernel,
        out_shape=[jax.ShapeDtypeStruct((b, t, d), F32), jax.ShapeDtypeStruct((b, t, d), BF16),
                   jax.ShapeDtypeStruct((b, N_EXPERTS, t), F32)],
        grid=(b, t // tm),
        in_specs=[tok(A_Q_DIM), tok(NA_DIM), tok(SSM_INNER), tok(N_BRANCH * d), tok(d),
                  const((A_Q_DIM, d)), const((NA_DIM, d)), const((SSM_INNER, d)), const((d, d)),
                  vec, const((1, d)), const((1, d)), vec, vec, const((N_EXPERTS, d))],
        out_specs=[tok(d), tok(d), pl.BlockSpec((1, N_EXPERTS, tm), lambda i, j: (i, 0, j))],
        compiler_params=_cparams(("parallel", "parallel")),
        name="merge_residual_router",
    )(ya, yb, yc, gates, hx, lw["w_a"], lw["w_b"], lw["w_c"], lw["w_out"], gm, lw["n_post"],
      lw["n_ffn_pre"], shf, scf, lw["w_rt"])


def _topk_kernel(cap, aff_ref, pos_o, gate_o, cs_o, eq_s, gt_s, ceq_s, csel_s):
    x = aff_ref[0]
    ne, t = x.shape
    bits = pltpu.bitcast(x, I32)
    theta = jnp.zeros((ne, 1), I32)
    for bit in range(30, -1, -1):
        cand = theta | (1 << bit)
        cnt = jnp.sum((bits >= cand).astype(I32), axis=-1, keepdims=True)
        theta = jnp.where(cnt >= cap, cand, theta)
    gt = bits > theta
    eq = bits == theta
    need = cap - jnp.sum(gt.astype(I32), axis=-1, keepdims=True)
    eq_s[...] = eq.astype(F32)
    gt_s[...] = gt.astype(F32)
    tb = ROUTE_TILE
    ii = lax.broadcasted_iota(I32, (tb, tb), 0)
    jj = lax.broadcasted_iota(I32, (tb, tb), 1)
    upper = (ii <= jj).astype(BF16)
    lane = lax.broadcasted_iota(I32, (ne, V7X_LANES), 1)
    needf = need.astype(F32)

    ceq_s[...] = jnp.zeros(ceq_s.shape, F32)
    csel_s[...] = jnp.zeros(csel_s.shape, F32)
    cs_o[0] = jnp.zeros((ne, V7X_LANES), I32)

    def body(k, carry):
        c_eq, c_sel = ceq_s[...], csel_s[...]
        sl = pl.ds(pl.multiple_of(k * tb, tb), tb)
        e_k = eq_s[:, sl]
        g_k = gt_s[:, sl]
        inc_eq = _dot(e_k.astype(BF16), upper)
        rank_eq = c_eq + inc_eq - e_k
        sel = g_k + e_k * (rank_eq < needf).astype(F32)
        inc_sel = _dot(sel.astype(BF16), upper)
        slot = c_sel + inc_sel - sel
        pos_o[0, :, sl] = jnp.where(sel > 0.5, slot, -1.0).astype(I32)
        gate_o[0, :, sl] = jnp.where(sel > 0.5, aff_ref[0, :, sl], 0.0)
        cs_o[0] = jnp.where(lane == k, c_sel.astype(I32), cs_o[0])
        ceq_s[...] = c_eq + inc_eq[:, tb - 1:tb]
        csel_s[...] = c_sel + inc_sel[:, tb - 1:tb]
        return carry

    nk = t // tb
    lax.fori_loop(0, nk, body, 0)
    cs_o[0] = jnp.where(lane == nk, csel_s[...].astype(I32), cs_o[0])


def _topk(aff_t, cap):
    b, ne, t = aff_t.shape
    assert t % ROUTE_TILE == 0 and t // ROUTE_TILE < V7X_LANES
    full = pl.BlockSpec((1, ne, t), lambda i: (i, 0, 0))
    return pl.pallas_call(
        functools.partial(_topk_kernel, cap),
        out_shape=[jax.ShapeDtypeStruct((b, ne, t), I32), jax.ShapeDtypeStruct((b, ne, t), F32),
                   jax.ShapeDtypeStruct((b, ne, V7X_LANES), I32)],
        grid=(b,),
        in_specs=[full],
        out_specs=[full, full, pl.BlockSpec((1, ne, V7X_LANES), lambda i: (i, 0, 0))],
        scratch_shapes=[pltpu.VMEM((ne, t), F32), pltpu.VMEM((ne, t), F32),
                        pltpu.VMEM((ne, 1), F32), pltpu.VMEM((ne, 1), F32)],
        compiler_params=_cparams(("parallel",)),
        name="expert_topk",
    )(aff_t)


def _cs_at(cs_ref, b, e, k):
    return cs_ref[(b * N_EXPERTS + e) * V7X_LANES + k]


_GATHER_SMALL = 64


_GATHER_TILES = 8


def _gather_kernel(cap, cs_ref, pos_ref, vx_ref, xs_o):
    b, e, g = pl.program_id(0), pl.program_id(1), pl.program_id(2)
    tt = ROUTE_TILE
    nt = vx_ref.shape[1] // tt
    narrow = min(2 * _GATHER_SMALL, cap)
    wide = min(_GATHER_SMALL + tt, cap)

    @pl.when(g == 0)
    def _():
        xs_o[...] = jnp.zeros(xs_o.shape, xs_o.dtype)

    def tile(i, carry):
        k = g * nt + i
        c0 = _cs_at(cs_ref, b, e, k)
        n_sel = _cs_at(cs_ref, b, e, k + 1) - c0
        toks = pl.ds(pl.multiple_of(i * tt, tt), tt)
        slot = pos_ref[0, 0, :, toks]

        def place(wn):
            w0 = jnp.minimum((c0 // _GATHER_SMALL) * _GATHER_SMALL, cap - wn)
            w0 = pl.multiple_of(w0, min(_GATHER_SMALL, cap))
            onehot = (lax.broadcasted_iota(I32, (wn, tt), 0) + w0 == slot).astype(BF16)
            rows = _dot(onehot, vx_ref[0, toks, :])
            win = xs_o[0, 0, pl.ds(w0, wn), :].astype(F32)
            xs_o[0, 0, pl.ds(w0, wn), :] = (win + rows).astype(xs_o.dtype)

        if narrow < wide:
            pl.when(jnp.logical_and(n_sel > 0, n_sel <= _GATHER_SMALL))(lambda: place(narrow))
            pl.when(n_sel > _GATHER_SMALL)(lambda: place(wide))
        else:
            pl.when(n_sel > 0)(lambda: place(wide))
        return carry

    lax.fori_loop(0, nt, tile, 0)


def _gather(cs_flat, pos_t, vx, cap):
    b, t, d = vx.shape
    ne = N_EXPERTS
    nt = min(_GATHER_TILES, t // ROUTE_TILE)
    tg = nt * ROUTE_TILE
    assert t % tg == 0 and (cap % _GATHER_SMALL == 0 or cap < _GATHER_SMALL)
    grid_spec = pltpu.PrefetchScalarGridSpec(
        num_scalar_prefetch=1,
        grid=(b, ne, t // tg),
        in_specs=[pl.BlockSpec((1, 1, 1, tg), lambda i, e, g, cs: (i, e, 0, g)),
                  pl.BlockSpec((1, tg, d), lambda i, e, g, cs: (i, g, 0))],
        out_specs=pl.BlockSpec((1, 1, cap, d), lambda i, e, g, cs: (i, e, 0, 0)))
    return pl.pallas_call(
        functools.partial(_gather_kernel, cap),
        out_shape=jax.ShapeDtypeStruct((b, ne, cap, d), BF16),
        grid_spec=grid_spec,
        compiler_params=_cparams(("parallel", "parallel", "arbitrary")),
        name="expert_gather",
    )(cs_flat, pos_t.reshape(b, ne, 1, t), vx)


_FFN_TILE_F = 512


def _ffn_kernel(x_ref, wg_ref, wu_ref, wd_ref, y_o, acc_s):
    j = pl.program_id(2)
    nb, _, cap, d = x_ref.shape
    x = x_ref[...].reshape(nb * cap, d)
    hid = _silu(_dot(x, wg_ref[0, 0].astype(BF16))) * _dot(x, wu_ref[0, 0].astype(BF16))
    part = _dot(hid.astype(BF16), wd_ref[0, 0].astype(BF16))

    @pl.when(j == 0)
    def _():
        acc_s[...] = part

    @pl.when(j > 0)
    def _():
        acc_s[...] += part

    @pl.when(j == pl.num_programs(2) - 1)
    def _():
        y_o[...] = acc_s[...].astype(y_o.dtype).reshape(y_o.shape)


_FFN_MAX_ROWS = 2048


def _expert_ffn(xs, layer, wg, wu, wd):
    b, ne, cap, d = xs.shape
    f = wg.shape[3]
    tf = _FFN_TILE_F
    nb = b if b * cap <= _FFN_MAX_ROWS else 1
    tok = pl.BlockSpec((nb, 1, cap, d), lambda e, i, j: (i, e, 0, 0))
    return pl.pallas_call(
        _ffn_kernel,
        out_shape=jax.ShapeDtypeStruct(xs.shape, BF16),
        grid=(ne, b // nb, f // tf),
        in_specs=[tok,
                  pl.BlockSpec((1, 1, d, tf), lambda e, i, j: (layer, e, 0, j)),
                  pl.BlockSpec((1, 1, d, tf), lambda e, i, j: (layer, e, 0, j)),
                  pl.BlockSpec((1, 1, tf, d), lambda e, i, j: (layer, e, j, 0))],
        out_specs=tok,
        scratch_shapes=[pltpu.VMEM((nb * cap, d), F32)],
        compiler_params=_cparams(("parallel", "parallel", "arbitrary")),
        name="expert_ffn",
    )(xs, wg, wu, wd)


_SLOT_ALIGN = 16


def _win_start(c0, cap, win):
    return jnp.minimum((c0 // _SLOT_ALIGN) * _SLOT_ALIGN, cap - win)


def _combine_kernel(cap, win, win_fast, cs_ref, *refs):
    ne = N_EXPERTS
    ys = refs[:ne]
    pos_ref, gate_ref, hx_ref, gf_ref, npost_ref, hx_o = refs[ne:]
    b, k = pl.program_id(0), pl.program_id(1)
    tt, d = hx_ref.shape[1], hx_ref.shape[2]
    pos, gate = pos_ref[0], gate_ref[0]
    starts, fits = [], None
    for e in range(ne):
        c0 = _cs_at(cs_ref, b, e, k)
        a = _win_start(c0, cap, win)
        starts.append(a)
        fit = _cs_at(cs_ref, b, e, k + 1) - a <= win_fast
        fits = fit if fits is None else jnp.logical_and(fits, fit)

    def run(width):
        lane = lax.broadcasted_iota(I32, (tt, width), 1)
        acc = jnp.zeros((tt, d), F32)
        for e in range(ne):
            local = pos[:, e:e + 1] - starts[e]
            onehot = jnp.where(lane == local, gate[:, e:e + 1], 0.0).astype(BF16)
            acc = acc + _dot(onehot, ys[e][0, 0, :width, :])
        hx_o[0] = hx_ref[0] + gf_ref[0] * _rms(acc, npost_ref[...])

    if win_fast < win:
        pl.when(fits)(lambda: run(win_fast))
        pl.when(jnp.logical_not(fits))(lambda: run(win))
    else:
        run(win)


def _combine(cs_flat, ys, pos_r, gate_r, hx, gf, n_post):
    b, t, d = hx.shape
    ne, cap = ys.shape[1], ys.shape[2]
    tt = ROUTE_TILE
    win = min(tt + _SLOT_ALIGN, cap)
    win_fast = min(tt, win)
    assert (cap - win) % _SLOT_ALIGN == 0

    def window(e):
        return pl.BlockSpec(
            (pl.Element(1), pl.Element(1), pl.Element(win), pl.Element(d)),
            lambda i, k, cs: (i, e, pl.multiple_of(_win_start(cs[(i * ne + e) * V7X_LANES + k], cap, win), _SLOT_ALIGN), 0))

    tok = lambda w: pl.BlockSpec((1, tt, w), lambda i, k, cs: (i, k, 0))
    grid_spec = pltpu.PrefetchScalarGridSpec(
        num_scalar_prefetch=1,
        grid=(b, t // tt),
        in_specs=[window(e) for e in range(ne)] + [
            tok(ne), tok(ne), tok(d),
            pl.BlockSpec((1, 1, d), lambda i, k, cs: (i, 0, 0)),
            pl.BlockSpec((1, d), lambda i, k, cs: (0, 0))],
        out_specs=tok(d))
    return pl.pallas_call(
        functools.partial(_combine_kernel, cap, win, win_fast),
        out_shape=jax.ShapeDtypeStruct((b, t, d), F32),
        grid_spec=grid_spec,
        compiler_params=_cparams(("parallel", "parallel")),
        name="expert_combine",
    )(cs_flat, *([ys] * ne), pos_r, gate_r, hx, gf, n_post)


def _moe(vx, aff_t, hx, gf, lw):
    b, t, d = hx.shape
    cap = EC_CAPACITY * t // N_EXPERTS
    pos_t, gate_t, cs = _topk(aff_t, cap)
    cs_flat = cs.reshape(-1)
    xs = _gather(cs_flat, pos_t, vx, cap)
    ys = _expert_ffn(xs, lw["layer"], lw["w_gate"], lw["w_up"], lw["w_down"])
    return _combine(cs_flat, ys, pos_t.transpose(0, 2, 1), gate_t.transpose(0, 2, 1), hx, gf,
                    lw["n_ffn_post"])


def _rope_tables(s):
    t = jnp.arange(s, dtype=I32)
    rows = (t // GRID_W).astype(F32)
    cols = (t % GRID_W).astype(F32)
    n_freq = HEAD_DIM // 4
    inv = ROPE_BASE ** (-jnp.arange(n_freq, dtype=F32) / n_freq)
    ar = rows[:, None] * inv[None, :]
    ac = cols[:, None] * inv[None, :]
    ang = jnp.concatenate([ar, ar, ac, ac] * (V7X_LANES // HEAD_DIM), axis=-1)
    return jnp.cos(ang), jnp.sin(ang)


def _layer_weights(i, p):
    d = p["w_in"].shape[1]
    w_in = p["w_in"][i]
    dt_pad = jnp.zeros((d, V7X_LANES - N_DIRS * SSM_HEADS), F32)
    hm = np.kron(np.eye(A_HEADS, dtype=np.float32), np.full((HEAD_DIM, HEAD_DIM), 1.0 / HEAD_DIM, np.float32))
    lane_pad = lambda v: jnp.pad(v.reshape(1, -1), ((0, 0), (0, V7X_LANES - v.size)))
    a = -jnp.exp(p["ssm_a_log"][i].astype(F32)).reshape(-1)
    return {
        "w_main": jnp.concatenate([w_in[:, :_DT_COL], w_in[:, _DT_COL + N_DIRS * SSM_HEADS:]], axis=1).astype(BF16),
        "w_dt": jnp.concatenate([w_in[:, _DT_COL:_DT_COL + N_DIRS * SSM_HEADS], dt_pad], axis=1).astype(BF16),
        "qn": jnp.tile(p["q_norm"][i], A_HEADS).reshape(1, -1),
        "kn": jnp.tile(p["k_norm"][i], A_KV_HEADS).reshape(1, -1),
        "hm": jnp.asarray(hm, BF16),
        "n_pre": p["norm_mix_pre"][i].reshape(1, d),
        "n_post": p["norm_mix_post"][i].reshape(1, d),
        "n_ffn_pre": p["norm_ffn_pre"][i].reshape(1, d),
        "n_ffn_post": p["norm_ffn_post"][i].reshape(1, d),
        "bias_tab": _na_bias_table(p["na_rel_bias"][i]),
        "conv_w": jnp.pad(p["conv_w"][i], ((0, 8 - SSM_CONV), (0, 0))),
        "conv_b": p["conv_b"][i].reshape(1, -1),
        "dt_b": lane_pad(p["ssm_dt_bias"][i]),
        "dt_bt": p["ssm_dt_bias"][i].reshape(-1, 1),
        "a_row": lane_pad(a),
        "a_col": a.reshape(-1, 1),
        "d_skip": jnp.repeat(p["ssm_d"][i, 0] + p["ssm_d"][i, 1], SSM_HEAD_DIM).reshape(1, -1),
        "ssm_norm": p["ssm_norm"][i].reshape(1, -1),
        "w_a": p["w_branch_a"][i].astype(BF16),
        "w_b": p["w_branch_b"][i].astype(BF16),
        "w_c": p["w_branch_c"][i].astype(BF16),
        "w_out": p["w_out"][i].astype(BF16),
        "w_rt": p["w_router"][i].T,
        "layer": i,
        "w_gate": p["w_exp_gate"],
        "w_up": p["w_exp_up"],
        "w_down": p["w_exp_down"],
    }


def _ssd_branch(px, pc, lw, with_ctx_out):
    zx, xbcx, dtx = px
    zc, xbcc, dtc = pc
    b = zx.shape[0]
    nd = N_DIRS * SSM_HEADS
    dttx = dtx[:, :, :nd].transpose(0, 2, 1)
    dttc = dtc[:, :, :nd].transpose(0, 2, 1)
    h0 = jnp.zeros((b, SSM_HEADS, SSM_HEAD_DIM, SSM_STATE), F32)
    y0c, hc0 = _ssd_pass(0, xbcc, dtc, dttc, lw, h0)
    yc, hc1 = _ssd_pass(1, xbcc, dtc, dttc, lw, h0, fin=(y0c, zc))
    y0x, _ = _ssd_pass(0, xbcx, dtx, dttx, lw, hc0)
    yx, _ = _ssd_pass(1, xbcx, dtx, dttx, lw, hc1, fin=(y0x, zx))
    return yx, (yc if with_ctx_out else None)


def _layer(i, hx, hc, mod_x, mod_c, p, rope_tabs, last):
    lw = _layer_weights(i, p)
    d = hx.shape[2]
    sh_m, sc_m, g_m, sh_f, sc_f, g_f = [mod_x[:, None, j * d:(j + 1) * d] for j in range(N_ADA)]
    csh_m, csc_m, cg_m, csh_f, csc_f, cg_f = [mod_c[:, None, j * d:(j + 1) * d] for j in range(N_ADA)]

    qa_c, ka_c, va_c, qb_c, kb_c, vb_c, z_c, xbc_c, dt_c, gt_c = _proj(hc, lw["n_pre"], csh_m, csc_m, lw, None)
    qa, ka, va, qb, kb, vb, z, xbc, dt, gt = _proj(hx, lw["n_pre"], sh_m, sc_m, lw, rope_tabs)

    ya = _flash(qa, jnp.concatenate([ka, ka_c], axis=1), jnp.concatenate([va, va_c], axis=1),
                A_KV_HEADS, A_HEADS // A_KV_HEADS, True)
    yb = _na(qb, kb, vb, kb_c, vb_c, lw["bias_tab"])
    yc, yc_c = _ssd_branch((z, xbc, dt), (z_c, xbc_c, dt_c), lw, not last)
    hx, vx, aff = _merge(ya, yb, yc, gt, hx, lw, g_m, sh_f, sc_f)
    hx = _moe(vx, aff, hx, g_f, lw)
    if last:
        return hx, hc
    ya_c = _flash(qa_c, ka_c, va_c, A_KV_HEADS, A_HEADS // A_KV_HEADS, True)
    yb_c = _flash(qb_c, kb_c, vb_c, NA_HEADS, 1, False)
    hc, vc, aff_c = _merge(ya_c, yb_c, yc_c, gt_c, hc, lw, cg_m, csh_f, csc_f)
    hc = _moe(vc, aff_c, hc, cg_f, lw)
    return hx, hc


def kernel(x, c, ctx, c_ctx, w_ada, b_ada, norm_mix_pre, norm_mix_post, norm_ffn_pre, norm_ffn_post, w_in, q_norm, k_norm, na_rel_bias, conv_w, conv_b, ssm_a_log, ssm_dt_bias, ssm_d, ssm_norm, w_branch_a, w_branch_b, w_branch_c, w_out, w_router, w_exp_gate, w_exp_up, w_exp_down):
    p = dict(norm_mix_pre=norm_mix_pre, norm_mix_post=norm_mix_post, norm_ffn_pre=norm_ffn_pre,
             norm_ffn_post=norm_ffn_post, w_in=w_in, q_norm=q_norm, k_norm=k_norm, na_rel_bias=na_rel_bias,
             conv_w=conv_w, conv_b=conv_b, ssm_a_log=ssm_a_log, ssm_dt_bias=ssm_dt_bias, ssm_d=ssm_d,
             ssm_norm=ssm_norm, w_branch_a=w_branch_a, w_branch_b=w_branch_b, w_branch_c=w_branch_c,
             w_out=w_out, w_router=w_router, w_exp_gate=w_exp_gate, w_exp_up=w_exp_up, w_exp_down=w_exp_down)
    b, s, d = x.shape
    depth = w_ada.shape[0]
    assert b + 1 <= 8
    cc = jnp.concatenate([c, c_ctx[None, :], jnp.zeros((8 - b - 1, d), F32)], axis=0)
    mod = _ada_mod(cc, w_ada, b_ada)
    rope_tabs = _rope_tables(s)
    hx, hc = x, ctx
    for i in range(depth):
        mod_x = mod[i, :b]
        mod_c = jnp.broadcast_to(mod[i, b:b + 1], (b, mod.shape[2]))
        hx, hc = _layer(i, hx, hc, mod_x, mod_c, p, rope_tabs, i == depth - 1)
    return hx
```

```python
import functools

import jax
import jax.numpy as jnp
import numpy as np
from jax import lax
from jax.experimental import pallas as pl
from jax.experimental.pallas import tpu as pltpu

F32, BF16, I32 = jnp.float32, jnp.bfloat16, jnp.int32
HIGHEST = lax.Precision.HIGHEST

GRID_W = 64
HEAD_DIM = 64
ROPE_BASE = 10000.0
NORM_EPS = 1e-6
N_ADA = 6
A_HEADS, A_KV_HEADS = 8, 2
NA_HEADS, NA_KH, NA_KW = 8, 8, 16
SSM_HEADS, SSM_HEAD_DIM, SSM_INNER = 8, 64, 512
SSM_GROUPS, SSM_STATE, SSM_CONV, N_DIRS = 2, 64, 5, 2
XBC_DIM = SSM_INNER + 2 * SSM_GROUPS * SSM_STATE
A_Q_DIM, A_KV_DIM, NA_DIM = A_HEADS * HEAD_DIM, A_KV_HEADS * HEAD_DIM, NA_HEADS * HEAD_DIM
N_EXPERTS, EC_CAPACITY = 16, 2
N_BRANCH = 3
ATT_SCALE = HEAD_DIM ** -0.5
LOG2_E = 1.4426950408889634

V7X_LANES = 128
V7X_VMEM_LIMIT_BYTES = 56 * 1024 * 1024
NEG_BIG = -1e30

SSD_CHUNK = 128
ROUTE_TILE = 256


def _cparams(sem):
    return pltpu.CompilerParams(dimension_semantics=sem, vmem_limit_bytes=V7X_VMEM_LIMIT_BYTES)


def _nt_dot(a, b, precision=None):
    return lax.dot_general(a, b, (((1,), (1,)), ((), ())), preferred_element_type=F32,
                           precision=precision)


def _dot(a, b, precision=None):
    return jnp.dot(a, b, preferred_element_type=F32, precision=precision)


def _silu(x):
    return x * jax.nn.sigmoid(x)


def _rms(x, gain):
    return x * lax.rsqrt(jnp.mean(x * x, axis=-1, keepdims=True) + NORM_EPS) * gain


def _ada_kernel(c_ref, w_ref, b_ref, o_ref):
    o_ref[0] = _dot(_silu(c_ref[...]), w_ref[0], HIGHEST) + b_ref[0]


def _ada_mod(cc, w_ada, b_ada):
    nl, d, n = w_ada.shape
    tn = 512
    return pl.pallas_call(
        _ada_kernel,
        out_shape=jax.ShapeDtypeStruct((nl, 8, n), F32),
        grid=(nl, n // tn),
        in_specs=[pl.BlockSpec((8, d), lambda l, j: (0, 0)),
                  pl.BlockSpec((1, d, tn), lambda l, j: (l, 0, j)),
                  pl.BlockSpec((1, 1, tn), lambda l, j: (l, 0, j))],
        out_specs=pl.BlockSpec((1, 8, tn), lambda l, j: (l, 0, j)),
        compiler_params=_cparams(("parallel", "parallel")),
        name="ada_mod",
    )(cc, w_ada, b_ada.reshape(nl, 1, n))


_OFF_QA, _OFF_KA, _OFF_VA, _OFF_QB, _OFF_KB, _OFF_VB, _OFF_Z, _OFF_XBC, _OFF_G, _OFF_END = (
    0, 512, 640, 768, 1280, 1792, 2304, 2816, 3584, 6656)
_DT_COL = 3584


def _head_rms(x, hm, gain):
    ms = _dot((x * x).astype(BF16), hm)
    return x * lax.rsqrt(ms + NORM_EPS) * gain


def _rope(x, cos, sin):
    w = x.shape[-1]
    lane = lax.broadcasted_iota(I32, x.shape, 1)
    first = (lane & 16) == 0
    rot = jnp.where(first, -pltpu.roll(x, w - 16, 1), pltpu.roll(x, 16, 1))
    return x * cos + rot * sin


def _proj_kernel(rope, h_ref, g_ref, sh_ref, sc_ref, w_ref, wdt_ref, qn_ref, kn_ref, hm_ref, *rest):
    if rope:
        cos_ref, sin_ref = rest[:2]
        rest = rest[2:]
    qa_o, ka_o, va_o, qb_o, kb_o, vb_o, z_o, xbc_o, dt_o, gt_o = rest
    u = _rms(h_ref[0], g_ref[...]) * (1.0 + sc_ref[0]) + sh_ref[0]
    ub = u.astype(BF16)

    def mm(lo, hi):
        return _dot(ub, w_ref[:, lo:hi])

    qa = _head_rms(mm(_OFF_QA, _OFF_KA), hm_ref[...], qn_ref[...])
    ka = _head_rms(mm(_OFF_KA, _OFF_VA), hm_ref[:A_KV_DIM, :A_KV_DIM], kn_ref[...])
    if rope:
        cs, sn = cos_ref[...], sin_ref[...]
        qa = _rope(qa, jnp.concatenate([cs] * 4, axis=1), jnp.concatenate([sn] * 4, axis=1))
        ka = _rope(ka, cs, sn)
    qa_o[0] = (qa * (ATT_SCALE * LOG2_E)).astype(BF16)
    ka_o[0] = ka.astype(BF16)
    va_o[0] = mm(_OFF_VA, _OFF_QB).astype(BF16)
    qb_o[0] = (mm(_OFF_QB, _OFF_KB) * ATT_SCALE).astype(BF16)
    kb_o[0] = mm(_OFF_KB, _OFF_VB).astype(BF16)
    vb_o[0] = mm(_OFF_VB, _OFF_Z).astype(BF16)
    z_o[0] = mm(_OFF_Z, _OFF_XBC)
    xbc_o[0] = mm(_OFF_XBC, _OFF_G)
    dt_o[0] = _dot(ub, wdt_ref[...])
    for i in range((_OFF_END - _OFF_G) // 512):
        gt_o[0, :, i * 512:(i + 1) * 512] = mm(_OFF_G + i * 512, _OFF_G + (i + 1) * 512)


def _proj(h, gain, shift, scale, lw, rope_tabs):
    b, t, d = h.shape
    tm = 256
    rope = rope_tabs is not None
    const = lambda shape: pl.BlockSpec(shape, lambda i, j: (0,) * len(shape))
    tok = lambda w: pl.BlockSpec((1, tm, w), lambda i, j: (i, j, 0))
    vec = pl.BlockSpec((1, 1, d), lambda i, j: (i, 0, 0))
    in_specs = [tok(d), const((1, d)), vec, vec, const((d, _OFF_END)), const((d, V7X_LANES)),
                const((1, A_Q_DIM)), const((1, A_KV_DIM)), const((A_Q_DIM, A_Q_DIM))]
    args = [h, gain, shift, scale, lw["w_main"], lw["w_dt"], lw["qn"], lw["kn"], lw["hm"]]
    if rope:
        tab = pl.BlockSpec((tm, V7X_LANES), lambda i, j: (j, 0))
        in_specs += [tab, tab]
        args += list(rope_tabs)
    widths = [(A_Q_DIM, BF16), (A_KV_DIM, BF16), (A_KV_DIM, BF16), (NA_DIM, BF16), (NA_DIM, BF16),
              (NA_DIM, BF16), (SSM_INNER, F32), (XBC_DIM, F32), (V7X_LANES, F32),
              (_OFF_END - _OFF_G, F32)]
    return pl.pallas_call(
        functools.partial(_proj_kernel, rope),
        out_shape=[jax.ShapeDtypeStruct((b, t, w), dt) for w, dt in widths],
        grid=(b, t // tm),
        in_specs=in_specs,
        out_specs=[tok(w) for w, _ in widths],
        compiler_params=_cparams(("parallel", "parallel")),
        name="norm_mod_proj",
    )(*args)


def _flash_kernel(nkv, grp, base2, q_ref, kt_ref, v_ref, o_ref, q_s, m_s, acc_s):
    j = pl.program_id(2)
    tq = q_ref.shape[1]
    tk = kt_ref.shape[2]
    hd = HEAD_DIM
    ex = jnp.exp2 if base2 else jnp.exp

    @pl.when(j == 0)
    def _():
        qb = q_ref[0]
        for g in range(nkv):
            parts = [qb[:, (g * grp + i) * hd:(g * grp + i + 1) * hd] for i in range(grp)]
            q_s[g] = parts[0] if grp == 1 else jnp.concatenate(parts, axis=0)
        m_s[...] = jnp.full(m_s.shape, -jnp.inf, F32)
        acc_s[...] = jnp.zeros(acc_s.shape, F32)

    for g in range(nkv):
        s = _dot(q_s[g], kt_ref[0, g * hd:(g + 1) * hd, :])
        m_prev = m_s[g]
        m_new = jnp.maximum(m_prev, jnp.max(s, axis=-1, keepdims=True))
        alpha = ex(m_prev - m_new)
        m_s[g] = m_new
        p = jnp.concatenate(
            [ex(s[:, c * V7X_LANES:(c + 1) * V7X_LANES] - m_new).astype(BF16) for c in range(tk // V7X_LANES)],
            axis=1)
        acc_s[g] = alpha * acc_s[g] + _dot(p, v_ref[0, :, g * V7X_LANES:(g + 1) * V7X_LANES])

    @pl.when(j == pl.num_programs(2) - 1)
    def _():
        for g in range(nkv):
            acc = acc_s[g]
            o = acc / pltpu.roll(acc, hd, 1)
            for i in range(grp):
                c0 = (g * grp + i) * hd
                o_ref[0, :, c0:c0 + hd] = o[i * tq:(i + 1) * tq, :hd].astype(o_ref.dtype)


def _pick_tile(n, cands):
    for c in cands:
        if n % c == 0:
            return c
    raise ValueError(f"no tile for {n}")


def _flash(q, k, v, nkv, grp, base2):
    b, s, qd = q.shape
    sk = k.shape[1]
    kt = k.transpose(0, 2, 1)
    ones = jnp.ones((b, sk, nkv, V7X_LANES - HEAD_DIM), v.dtype)
    vaug = jnp.concatenate([v.reshape(b, sk, nkv, HEAD_DIM), ones], axis=-1).reshape(b, sk, nkv * V7X_LANES)
    tq = _pick_tile(s, (512, 256, 128))
    tk = _pick_tile(sk, (3328, 1280, 1024, 512, 256, 128))
    rows = grp * tq
    return pl.pallas_call(
        functools.partial(_flash_kernel, nkv, grp, base2),
        out_shape=jax.ShapeDtypeStruct((b, s, qd), BF16),
        grid=(b, s // tq, sk // tk),
        in_specs=[pl.BlockSpec((1, tq, qd), lambda i, m, j: (i, m, 0)),
                  pl.BlockSpec((1, nkv * HEAD_DIM, tk), lambda i, m, j: (i, 0, j)),
                  pl.BlockSpec((1, tk, nkv * V7X_LANES), lambda i, m, j: (i, j, 0))],
        out_specs=pl.BlockSpec((1, tq, qd), lambda i, m, j: (i, m, 0)),
        scratch_shapes=[pltpu.VMEM((nkv, rows, HEAD_DIM), BF16),
                        pltpu.VMEM((nkv, rows, V7X_LANES), F32),
                        pltpu.VMEM((nkv, rows, V7X_LANES), F32)],
        compiler_params=_cparams(("parallel", "parallel", "arbitrary")),
        name=f"flash_attn_{nkv}x{grp}",
    )(q, kt, vaug)


_NA_ROWS_PER_STEP = 8
_NA_BLOCK = _NA_ROWS_PER_STEP * GRID_W
_NA_BAND = NA_KH * GRID_W


def _na_kernel(rows, q_ref, kp_ref, kc_ref, kn_ref, vp_ref, vc_ref, vn_ref, kx_ref, vx_ref, bias_ref,
               o_ref, kcat_s, vcat_s):
    g = pl.program_id(1)
    nb = _NA_BLOCK
    for i, (kr, vr) in enumerate(((kp_ref, vp_ref), (kc_ref, vc_ref), (kn_ref, vn_ref))):
        kcat_s[i * nb:(i + 1) * nb, :] = kr[0]
        vcat_s[i * nb:(i + 1) * nb, :] = vr[0]
    n_ctx = kx_ref.shape[1]
    lanes = V7X_LANES
    head0 = lax.broadcasted_iota(I32, (GRID_W, lanes), 1) < HEAD_DIM
    ones_loc = jnp.ones((_NA_BAND, lanes), BF16)
    ones_ctx = jnp.ones((n_ctx, lanes), BF16)

    def body(j, carry):
        r = g * _NA_ROWS_PER_STEP + j
        r0 = jnp.clip(r - NA_KH // 2, 0, rows - NA_KH)
        off = pl.multiple_of((r0 - g * _NA_ROWS_PER_STEP + _NA_ROWS_PER_STEP) * GRID_W, GRID_W)
        case = r - r0
        qrows = pl.ds(pl.multiple_of(j * GRID_W, GRID_W), GRID_W)
        for pr in range(NA_HEADS // 2):
            ls = slice(pr * lanes, (pr + 1) * lanes)
            qp = q_ref[0, qrows, ls]
            zero = jnp.zeros_like(qp)
            q2 = jnp.concatenate([jnp.where(head0, qp, zero), jnp.where(head0, zero, qp)], axis=0)
            kband = kcat_s[pl.ds(off, _NA_BAND), ls]
            vband = vcat_s[pl.ds(off, _NA_BAND), ls]
            s_loc = _nt_dot(q2, kband) + bias_ref[case * (NA_HEADS // 2) + pr]
            s_ctx = _nt_dot(q2, kx_ref[0, :, ls])
            m = jnp.maximum(jnp.max(s_loc, axis=-1, keepdims=True), jnp.max(s_ctx, axis=-1, keepdims=True))
            p_loc = jnp.exp(s_loc - m).astype(BF16)
            p_ctx = jnp.exp(s_ctx - m).astype(BF16)
            pvl = (_dot(p_loc, jnp.concatenate([vband, ones_loc], axis=1))
                   + _dot(p_ctx, jnp.concatenate([vx_ref[0, :, ls], ones_ctx], axis=1)))
            o2 = pvl[:, :lanes] / pvl[:, lanes:]
            o_ref[0, qrows, ls] = jnp.where(head0, o2[:GRID_W], o2[GRID_W:]).astype(o_ref.dtype)
        return carry

    lax.fori_loop(0, _NA_ROWS_PER_STEP, body, 0, unroll=4)


def _na_bias_table(rel_bias):
    nh, nr, nc = rel_bias.shape
    col = np.arange(GRID_W)
    c0 = np.clip(col - NA_KW // 2, 0, GRID_W - NA_KW)
    valid = (col[None, :] >= c0[:, None]) & (col[None, :] < c0[:, None] + NA_KW)
    w2 = 2 * GRID_W
    lo = GRID_W - NA_KW
    padded = jnp.pad(rel_bias.astype(F32), ((0, 0), (0, 0), (lo, w2 - lo - nc)))
    flat = jnp.broadcast_to(padded[:, :, None, :], (nh, nr, GRID_W, w2)).reshape(nh, nr, GRID_W * w2)
    skew = flat[:, :, :GRID_W * (w2 - 1)].reshape(nh, nr, GRID_W, w2 - 1)
    toe = skew[:, :, :, GRID_W - 1:]
    tab = jnp.stack([toe[:, NA_KH - 1 - case:2 * NA_KH - 1 - case] for case in range(NA_KH)])
    tab = jnp.where(valid[None, None, None], tab, NEG_BIG)
    tab = tab.transpose(0, 1, 3, 2, 4)
    return tab.reshape(NA_KH * NA_HEADS // 2, 2 * GRID_W, NA_KH * GRID_W)


def _na(q, k, v, kx, vx, bias_tab):
    b, s, d = q.shape
    l = kx.shape[1]
    rows = s // GRID_W
    assert rows % _NA_ROWS_PER_STEP == 0 and rows >= NA_KH
    ng = rows // _NA_ROWS_PER_STEP
    nb = _NA_BLOCK
    cur = pl.BlockSpec((1, nb, d), lambda i, g: (i, g, 0))
    prv = pl.BlockSpec((1, nb, d), lambda i, g: (i, jnp.maximum(g - 1, 0), 0))
    nxt = pl.BlockSpec((1, nb, d), lambda i, g: (i, jnp.minimum(g + 1, ng - 1), 0))
    ctx = pl.BlockSpec((1, l, d), lambda i, g: (i, 0, 0))
    return pl.pallas_call(
        functools.partial(_na_kernel, rows),
        out_shape=jax.ShapeDtypeStruct((b, s, d), BF16),
        grid=(b, ng),
        in_specs=[cur, prv, cur, nxt, prv, cur, nxt, ctx, ctx,
                  pl.BlockSpec(bias_tab.shape, lambda i, g: (0, 0, 0))],
        out_specs=cur,
        scratch_shapes=[pltpu.VMEM((3 * nb, d), BF16), pltpu.VMEM((3 * nb, d), BF16)],
        compiler_params=_cparams(("parallel", "parallel")),
        name="neighbourhood_attn",
    )(q, k, k, k, v, v, v, kx, vx, bias_tab)


def _softplus(x):
    return jnp.maximum(x, 0.0) + jnp.log1p(jnp.exp(-jnp.abs(x)))


def _ssd_kernel(d, final, xc_ref, xp_ref, xn_ref, dt_ref, dtt_ref, cw_ref, cb_ref, dtb_ref, dtbt_ref,
                a_ref, at_ref, h0_ref, *rest):
    if final:
        y0_ref, z_ref, dsk_ref, ng_ref, y_o, h_o, h_s, xcat_s = rest
    else:
        y0_ref = z_ref = dsk_ref = ng_ref = None
        y_o, h_o, h_s, xcat_s = rest
    t = pl.program_id(0)
    nc = pl.num_programs(0)
    reverse = d == 1
    c = nc - 1 - t if reverse else t
    q = xc_ref.shape[1]

    @pl.when(t == 0)
    def _():
        h_s[...] = h0_ref[...]

    pm = (c > 0).astype(F32)
    nm = (c < nc - 1).astype(F32)
    ii = lax.broadcasted_iota(I32, (q, q), 0)
    jj = lax.broadcasted_iota(I32, (q, q), 1)
    keep = (jj >= ii) if reverse else (jj <= ii)
    tri = keep.astype(F32)
    chan = lax.broadcasted_iota(I32, (V7X_LANES, SSM_INNER), 1)
    dt_col = lax.broadcasted_iota(I32, (V7X_LANES, SSM_INNER), 0)
    expand = (dt_col == d * SSM_HEADS + lax.shift_right_logical(chan, 6)).astype(F32)
    for bi in range(xc_ref.shape[0]):
        _ssd_chunk(d, bi, pm, nm, keep, tri, expand, xc_ref, xp_ref, xn_ref, dt_ref, dtt_ref, cw_ref, cb_ref,
                   dtb_ref, dtbt_ref, a_ref, at_ref, y0_ref, z_ref, dsk_ref, ng_ref, y_o, h_s, xcat_s)

    @pl.when(t == nc - 1)
    def _():
        h_o[...] = h_s[...]


def _ssd_chunk(d, bi, pm, nm, keep, tri, expand, xc_ref, xp_ref, xn_ref, dt_ref, dtt_ref, cw_ref, cb_ref,
               dtb_ref, dtbt_ref, a_ref, at_ref, y0_ref, z_ref, dsk_ref, ng_ref, y_o, h_s, xcat_s):
    final = y0_ref is not None
    reverse = d == 1
    q = xc_ref.shape[1]
    hp = SSM_HEAD_DIM
    ns = SSM_STATE

    xcat_s[bi, 0:8, :] = xp_ref[bi] * pm
    xcat_s[bi, 8:8 + q, :] = xc_ref[bi]
    xcat_s[bi, 8 + q:16 + q, :] = xn_ref[bi] * nm
    acc = jnp.zeros((q, XBC_DIM), F32)
    for k in range(SSM_CONV):
        acc = acc + xcat_s[bi, 8 - SSM_CONV // 2 + k:8 - SSM_CONV // 2 + k + q, :] * cw_ref[k:k + 1, :]
    u = _silu(acc + cb_ref[...])
    xs = u[:, :SSM_INNER]
    bm = u[:, SSM_INNER:SSM_INNER + SSM_GROUPS * ns]
    cm = u[:, SSM_INNER + SSM_GROUPS * ns:]

    dt = _softplus(dt_ref[bi] + dtb_ref[...])
    dtt = _softplus(dtt_ref[bi] + dtbt_ref[...])
    acs = _dot(tri, dt * a_ref[...], HIGHEST)
    acst = _nt_dot(dtt * at_ref[...], tri, HIGHEST)
    last = 0 if reverse else q - 1

    xdt_all = xs * _dot(dt, expand, HIGHEST)
    xdt_t = [xdt_all[:, p * V7X_LANES:(p + 1) * V7X_LANES].T for p in range(SSM_INNER // V7X_LANES)]
    xdt_b = xdt_all.astype(BF16)
    gmats = []
    for gi in range(SSM_GROUPS):
        gmats.append(_nt_dot(cm[:, gi * ns:(gi + 1) * ns].astype(BF16),
                             bm[:, gi * ns:(gi + 1) * ns].astype(BF16)))
    ys = []
    for h in range(SSM_HEADS):
        col = d * SSM_HEADS + h
        gi = h // (SSM_HEADS // SSM_GROUPS)
        a_col = acs[:, col:col + 1]
        a_row = acst[col:col + 1, :]
        tot = a_col[last:last + 1, :]
        lmat = jnp.exp(jnp.where(keep, a_col - a_row, NEG_BIG))
        y_diag = _dot((gmats[gi] * lmat).astype(BF16), xdt_b[:, h * hp:(h + 1) * hp])
        hst = h_s[bi, h]
        cg = cm[:, gi * ns:(gi + 1) * ns]
        bg = bm[:, gi * ns:(gi + 1) * ns]
        y_off = _nt_dot((cg * jnp.exp(a_col)).astype(BF16), hst.astype(BF16))
        ys.append(y_diag + y_off)
        bdec = (bg * jnp.exp(tot - a_col)).astype(BF16)
        xt = xdt_t[h // 2][(h % 2) * hp:(h % 2 + 1) * hp, :]
        h_s[bi, h] = jnp.exp(tot) * hst + _dot(xt.astype(BF16), bdec)
    y = jnp.concatenate(ys, axis=1)

    if final:
        yt = y0_ref[bi] + y + dsk_ref[...] * xs
        y_o[bi] = _rms(yt * _silu(z_ref[bi]), ng_ref[...]).astype(y_o.dtype)
    else:
        y_o[bi] = y


def _ssd_pass(d, xbc, dt, dtt, lw, h0, fin=None):
    b, l, _ = xbc.shape
    q = SSD_CHUNK
    nc = l // q
    final = fin is not None
    cidx = (lambda t: nc - 1 - t) if d == 1 else (lambda t: t)
    r8 = q // 8
    nb8 = l // 8
    const = lambda shape: pl.BlockSpec(shape, lambda t: (0,) * len(shape))
    tok = lambda w: pl.BlockSpec((b, q, w), lambda t: (0, cidx(t), 0))
    hspec = pl.BlockSpec((b, SSM_HEADS, SSM_HEAD_DIM, SSM_STATE), lambda t: (0, 0, 0, 0))
    in_specs = [
        tok(XBC_DIM),
        pl.BlockSpec((b, 8, XBC_DIM), lambda t: (0, jnp.maximum(cidx(t) * r8 - 1, 0), 0)),
        pl.BlockSpec((b, 8, XBC_DIM), lambda t: (0, jnp.minimum((cidx(t) + 1) * r8, nb8 - 1), 0)),
        tok(V7X_LANES),
        pl.BlockSpec((b, 16, q), lambda t: (0, 0, cidx(t))),
        const((8, XBC_DIM)), const((1, XBC_DIM)), const((1, V7X_LANES)), const((16, 1)),
        const((1, V7X_LANES)), const((16, 1)), hspec]
    args = [xbc, xbc, xbc, dt, dtt, lw["conv_w"], lw["conv_b"], lw["dt_b"], lw["dt_bt"],
            lw["a_row"], lw["a_col"], h0]
    if final:
        in_specs += [tok(SSM_INNER), tok(SSM_INNER), const((1, SSM_INNER)), const((1, SSM_INNER))]
        args += [fin[0], fin[1], lw["d_skip"], lw["ssm_norm"]]
    return pl.pallas_call(
        functools.partial(_ssd_kernel, d, final),
        out_shape=[jax.ShapeDtypeStruct((b, l, SSM_INNER), BF16 if final else F32),
                   jax.ShapeDtypeStruct(h0.shape, F32)],
        grid=(nc,),
        in_specs=in_specs,
        out_specs=[tok(SSM_INNER), hspec],
        scratch_shapes=[pltpu.VMEM((b, SSM_HEADS, SSM_HEAD_DIM, SSM_STATE), F32),
                        pltpu.VMEM((b, q + 16, XBC_DIM), F32)],
        compiler_params=_cparams(("arbitrary",)),
        name=f"ssd_scan_d{d}{'_final' if final else ''}",
    )(*args)


def _merge_kernel(ya_ref, yb_ref, yc_ref, gt_ref, hx_ref, wa_ref, wb_ref, wc_ref, wo_ref, gm_ref,
                  npost_ref, npre_ref, shf_ref, scf_ref, wr_ref, hx_o, vx_o, aff_o):
    d = hx_ref.shape[2]
    gt = gt_ref[0]
    m = (jax.nn.sigmoid(gt[:, :d]) * _dot(ya_ref[0], wa_ref[...])
         + jax.nn.sigmoid(gt[:, d:2 * d]) * _dot(yb_ref[0], wb_ref[...])
         + jax.nn.sigmoid(gt[:, 2 * d:]) * _dot(yc_ref[0], wc_ref[...]))
    out = _dot(m.astype(BF16), wo_ref[...])
    hx = hx_ref[0] + gm_ref[0] * _rms(out, npost_ref[...])
    hx_o[0] = hx
    v = _rms(hx, npre_ref[...]) * (1.0 + scf_ref[0]) + shf_ref[0]
    vx_o[0] = v.astype(BF16)
    logits = _nt_dot(wr_ref[...], v, HIGHEST)
    e = jnp.exp(logits - jnp.max(logits, axis=0, keepdims=True))
    aff_o[0] = e / jnp.sum(e, axis=0, keepdims=True)


def _merge(ya, yb, yc, gates, hx, lw, gm, shf, scf):
    b, t, d = hx.shape
    tm = _pick_tile(t, (512, 256))
    const = lambda shape: pl.BlockSpec(shape, lambda i, j: (0,) * len(shape))
    tok = lambda w: pl.BlockSpec((1, tm, w), lambda i, j: (i, j, 0))
    vec = pl.BlockSpec((1, 1, d), lambda i, j: (i, 0, 0))
    return pl.pallas_call(
        _merge_kernel,
        out_shape=[jax.ShapeDtypeStruct((b, t, d), F32), jax.ShapeDtypeStruct((b, t, d), BF16),
                   jax.ShapeDtypeStruct((b, N_EXPERTS, t), F32)],
        grid=(b, t // tm),
        in_specs=[tok(A_Q_DIM), tok(NA_DIM), tok(SSM_INNER), tok(N_BRANCH * d), tok(d),
                  const((A_Q_DIM, d)), const((NA_DIM, d)), const((SSM_INNER, d)), const((d, d)),
                  vec, const((1, d)), const((1, d)), vec, vec, const((N_EXPERTS, d))],
        out_specs=[tok(d), tok(d), pl.BlockSpec((1, N_EXPERTS, tm), lambda i, j: (i, 0, j))],
        compiler_params=_cparams(("parallel", "parallel")),
        name="merge_residual_router",
    )(ya, yb, yc, gates, hx, lw["w_a"], lw["w_b"], lw["w_c"], lw["w_out"], gm, lw["n_post"],
      lw["n_ffn_pre"], shf, scf, lw["w_rt"])


def _topk_kernel(cap, aff_ref, pos_o, gate_o, cs_o, eq_s, gt_s, ceq_s, csel_s):
    x = aff_ref[0]
    ne, t = x.shape
    bits = pltpu.bitcast(x, I32)
    theta = jnp.zeros((ne, 1), I32)
    for bit in range(30, -1, -1):
        cand = theta | (1 << bit)
        cnt = jnp.sum((bits >= cand).astype(I32), axis=-1, keepdims=True)
        theta = jnp.where(cnt >= cap, cand, theta)
    gt = bits > theta
    eq = bits == theta
    need = cap - jnp.sum(gt.astype(I32), axis=-1, keepdims=True)
    eq_s[...] = eq.astype(F32)
    gt_s[...] = gt.astype(F32)
    tb = ROUTE_TILE
    ii = lax.broadcasted_iota(I32, (tb, tb), 0)
    jj = lax.broadcasted_iota(I32, (tb, tb), 1)
    upper = (ii <= jj).astype(BF16)
    lane = lax.broadcasted_iota(I32, (ne, V7X_LANES), 1)
    needf = need.astype(F32)

    ceq_s[...] = jnp.zeros(ceq_s.shape, F32)
    csel_s[...] = jnp.zeros(csel_s.shape, F32)
    cs_o[0] = jnp.zeros((ne, V7X_LANES), I32)

    def body(k, carry):
        c_eq, c_sel = ceq_s[...], csel_s[...]
        sl = pl.ds(pl.multiple_of(k * tb, tb), tb)
        e_k = eq_s[:, sl]
        g_k = gt_s[:, sl]
        inc_eq = _dot(e_k.astype(BF16), upper)
        rank_eq = c_eq + inc_eq - e_k
        sel = g_k + e_k * (rank_eq < needf).astype(F32)
        inc_sel = _dot(sel.astype(BF16), upper)
        slot = c_sel + inc_sel - sel
        pos_o[0, :, sl] = jnp.where(sel > 0.5, slot, -1.0).astype(I32)
        gate_o[0, :, sl] = jnp.where(sel > 0.5, aff_ref[0, :, sl], 0.0)
        cs_o[0] = jnp.where(lane == k, c_sel.astype(I32), cs_o[0])
        ceq_s[...] = c_eq + inc_eq[:, tb - 1:tb]
        csel_s[...] = c_sel + inc_sel[:, tb - 1:tb]
        return carry

    nk = t // tb
    lax.fori_loop(0, nk, body, 0)
    cs_o[0] = jnp.where(lane == nk, csel_s[...].astype(I32), cs_o[0])


def _topk(aff_t, cap):
    b, ne, t = aff_t.shape
    assert t % ROUTE_TILE == 0 and t // ROUTE_TILE < V7X_LANES
    full = pl.BlockSpec((1, ne, t), lambda i: (i, 0, 0))
    return pl.pallas_call(
        functools.partial(_topk_kernel, cap),
        out_shape=[jax.ShapeDtypeStruct((b, ne, t), I32), jax.ShapeDtypeStruct((b, ne, t), F32),
                   jax.ShapeDtypeStruct((b, ne, V7X_LANES), I32)],
        grid=(b,),
        in_specs=[full],
        out_specs=[full, full, pl.BlockSpec((1, ne, V7X_LANES), lambda i: (i, 0, 0))],
        scratch_shapes=[pltpu.VMEM((ne, t), F32), pltpu.VMEM((ne, t), F32),
                        pltpu.VMEM((ne, 1), F32), pltpu.VMEM((ne, 1), F32)],
        compiler_params=_cparams(("parallel",)),
        name="expert_topk",
    )(aff_t)


def _cs_at(cs_ref, b, e, k):
    return cs_ref[(b * N_EXPERTS + e) * V7X_LANES + k]


_GATHER_SMALL = 64


_GATHER_TILES = 8


def _gather_kernel(cap, cs_ref, pos_ref, vx_ref, xs_o):
    b, e, g = pl.program_id(0), pl.program_id(1), pl.program_id(2)
    tt = ROUTE_TILE
    nt = vx_ref.shape[1] // tt
    narrow = min(2 * _GATHER_SMALL, cap)
    wide = min(_GATHER_SMALL + tt, cap)

    @pl.when(g == 0)
    def _():
        xs_o[...] = jnp.zeros(xs_o.shape, xs_o.dtype)

    def tile(i, carry):
        k = g * nt + i
        c0 = _cs_at(cs_ref, b, e, k)
        n_sel = _cs_at(cs_ref, b, e, k + 1) - c0
        toks = pl.ds(pl.multiple_of(i * tt, tt), tt)
        slot = pos_ref[0, 0, :, toks]

        def place(wn):
            w0 = jnp.minimum((c0 // _GATHER_SMALL) * _GATHER_SMALL, cap - wn)
            w0 = pl.multiple_of(w0, min(_GATHER_SMALL, cap))
            onehot = (lax.broadcasted_iota(I32, (wn, tt), 0) + w0 == slot).astype(BF16)
            rows = _dot(onehot, vx_ref[0, toks, :])
            win = xs_o[0, 0, pl.ds(w0, wn), :].astype(F32)
            xs_o[0, 0, pl.ds(w0, wn), :] = (win + rows).astype(xs_o.dtype)

        if narrow < wide:
            pl.when(jnp.logical_and(n_sel > 0, n_sel <= _GATHER_SMALL))(lambda: place(narrow))
            pl.when(n_sel > _GATHER_SMALL)(lambda: place(wide))
        else:
            pl.when(n_sel > 0)(lambda: place(wide))
        return carry

    lax.fori_loop(0, nt, tile, 0)


def _gather(cs_flat, pos_t, vx, cap):
    b, t, d = vx.shape
    ne = N_EXPERTS
    nt = min(_GATHER_TILES, t // ROUTE_TILE)
    tg = nt * ROUTE_TILE
    assert t % tg == 0 and (cap % _GATHER_SMALL == 0 or cap < _GATHER_SMALL)
    grid_spec = pltpu.PrefetchScalarGridSpec(
        num_scalar_prefetch=1,
        grid=(b, ne, t // tg),
        in_specs=[pl.BlockSpec((1, 1, 1, tg), lambda i, e, g, cs: (i, e, 0, g)),
                  pl.BlockSpec((1, tg, d), lambda i, e, g, cs: (i, g, 0))],
        out_specs=pl.BlockSpec((1, 1, cap, d), lambda i, e, g, cs: (i, e, 0, 0)))
    return pl.pallas_call(
        functools.partial(_gather_kernel, cap),
        out_shape=jax.ShapeDtypeStruct((b, ne, cap, d), BF16),
        grid_spec=grid_spec,
        compiler_params=_cparams(("parallel", "parallel", "arbitrary")),
        name="expert_gather",
    )(cs_flat, pos_t.reshape(b, ne, 1, t), vx)


_FFN_TILE_F = 512


def _ffn_kernel(x_ref, wg_ref, wu_ref, wd_ref, y_o, acc_s):
    j = pl.program_id(2)
    nb, _, cap, d = x_ref.shape
    x = x_ref[...].reshape(nb * cap, d)
    hid = _silu(_dot(x, wg_ref[0, 0].astype(BF16))) * _dot(x, wu_ref[0, 0].astype(BF16))

    @pl.when(j == 0)
    def _():
        acc_s[...] = jnp.zeros(acc_s.shape, F32)

    acc_s[...] += _dot(hid.astype(BF16), wd_ref[0, 0].astype(BF16))

    @pl.when(j == pl.num_programs(2) - 1)
    def _():
        y_o[...] = acc_s[...].astype(y_o.dtype).reshape(y_o.shape)


_FFN_MAX_ROWS = 2048


def _expert_ffn(xs, layer, wg, wu, wd):
    b, ne, cap, d = xs.shape
    f = wg.shape[3]
    tf = _FFN_TILE_F
    nb = b if b * cap <= _FFN_MAX_ROWS else 1
    tok = pl.BlockSpec((nb, 1, cap, d), lambda e, i, j: (i, e, 0, 0))
    return pl.pallas_call(
        _ffn_kernel,
        out_shape=jax.ShapeDtypeStruct(xs.shape, BF16),
        grid=(ne, b // nb, f // tf),
        in_specs=[tok,
                  pl.BlockSpec((1, 1, d, tf), lambda e, i, j: (layer, e, 0, j)),
                  pl.BlockSpec((1, 1, d, tf), lambda e, i, j: (layer, e, 0, j)),
                  pl.BlockSpec((1, 1, tf, d), lambda e, i, j: (layer, e, j, 0))],
        out_specs=tok,
        scratch_shapes=[pltpu.VMEM((nb * cap, d), F32)],
        compiler_params=_cparams(("parallel", "parallel", "arbitrary")),
        name="expert_ffn",
    )(xs, wg, wu, wd)


_SLOT_ALIGN = 16


def _win_start(c0, cap, win):
    return jnp.minimum((c0 // _SLOT_ALIGN) * _SLOT_ALIGN, cap - win)


def _combine_kernel(cap, win, win_fast, cs_ref, *refs):
    ne = N_EXPERTS
    ys = refs[:ne]
    pos_ref, gate_ref, hx_ref, gf_ref, npost_ref, hx_o = refs[ne:]
    b, k = pl.program_id(0), pl.program_id(1)
    tt, d = hx_ref.shape[1], hx_ref.shape[2]
    pos, gate = pos_ref[0], gate_ref[0]
    starts, fits = [], None
    for e in range(ne):
        c0 = _cs_at(cs_ref, b, e, k)
        a = _win_start(c0, cap, win)
        starts.append(a)
        fit = _cs_at(cs_ref, b, e, k + 1) - a <= win_fast
        fits = fit if fits is None else jnp.logical_and(fits, fit)

    def run(width):
        lane = lax.broadcasted_iota(I32, (tt, width), 1)
        acc = jnp.zeros((tt, d), F32)
        for e in range(ne):
            local = pos[:, e:e + 1] - starts[e]
            onehot = jnp.where(lane == local, gate[:, e:e + 1], 0.0).astype(BF16)
            acc = acc + _dot(onehot, ys[e][0, 0, :width, :])
        hx_o[0] = hx_ref[0] + gf_ref[0] * _rms(acc, npost_ref[...])

    if win_fast < win:
        pl.when(fits)(lambda: run(win_fast))
        pl.when(jnp.logical_not(fits))(lambda: run(win))
    else:
        run(win)


def _combine(cs_flat, ys, pos_r, gate_r, hx, gf, n_post):
    b, t, d = hx.shape
    ne, cap = ys.shape[1], ys.shape[2]
    tt = ROUTE_TILE
    win = min(tt + _SLOT_ALIGN, cap)
    win_fast = min(tt, win)
    assert (cap - win) % _SLOT_ALIGN == 0

    def window(e):
        return pl.BlockSpec(
            (pl.Element(1), pl.Element(1), pl.Element(win), pl.Element(d)),
            lambda i, k, cs: (i, e, pl.multiple_of(_win_start(cs[(i * ne + e) * V7X_LANES + k], cap, win), _SLOT_ALIGN), 0))

    tok = lambda w: pl.BlockSpec((1, tt, w), lambda i, k, cs: (i, k, 0))
    grid_spec = pltpu.PrefetchScalarGridSpec(
        num_scalar_prefetch=1,
        grid=(b, t // tt),
        in_specs=[window(e) for e in range(ne)] + [
            tok(ne), tok(ne), tok(d),
            pl.BlockSpec((1, 1, d), lambda i, k, cs: (i, 0, 0)),
            pl.BlockSpec((1, d), lambda i, k, cs: (0, 0))],
        out_specs=tok(d))
    return pl.pallas_call(
        functools.partial(_combine_kernel, cap, win, win_fast),
        out_shape=jax.ShapeDtypeStruct((b, t, d), F32),
        grid_spec=grid_spec,
        compiler_params=_cparams(("parallel", "parallel")),
        name="expert_combine",
    )(cs_flat, *([ys] * ne), pos_r, gate_r, hx, gf, n_post)


def _moe(vx, aff_t, hx, gf, lw):
    b, t, d = hx.shape
    cap = EC_CAPACITY * t // N_EXPERTS
    pos_t, gate_t, cs = _topk(aff_t, cap)
    cs_flat = cs.reshape(-1)
    xs = _gather(cs_flat, pos_t, vx, cap)
    ys = _expert_ffn(xs, lw["layer"], lw["w_gate"], lw["w_up"], lw["w_down"])
    return _combine(cs_flat, ys, pos_t.transpose(0, 2, 1), gate_t.transpose(0, 2, 1), hx, gf,
                    lw["n_ffn_post"])


def _rope_tables(s):
    t = jnp.arange(s, dtype=I32)
    rows = (t // GRID_W).astype(F32)
    cols = (t % GRID_W).astype(F32)
    n_freq = HEAD_DIM // 4
    inv = ROPE_BASE ** (-jnp.arange(n_freq, dtype=F32) / n_freq)
    ar = rows[:, None] * inv[None, :]
    ac = cols[:, None] * inv[None, :]
    ang = jnp.concatenate([ar, ar, ac, ac] * (V7X_LANES // HEAD_DIM), axis=-1)
    return jnp.cos(ang), jnp.sin(ang)


def _layer_weights(i, p):
    d = p["w_in"].shape[1]
    w_in = p["w_in"][i]
    dt_pad = jnp.zeros((d, V7X_LANES - N_DIRS * SSM_HEADS), F32)
    hm = np.kron(np.eye(A_HEADS, dtype=np.float32), np.full((HEAD_DIM, HEAD_DIM), 1.0 / HEAD_DIM, np.float32))
    lane_pad = lambda v: jnp.pad(v.reshape(1, -1), ((0, 0), (0, V7X_LANES - v.size)))
    a = -jnp.exp(p["ssm_a_log"][i].astype(F32)).reshape(-1)
    return {
        "w_main": jnp.concatenate([w_in[:, :_DT_COL], w_in[:, _DT_COL + N_DIRS * SSM_HEADS:]], axis=1).astype(BF16),
        "w_dt": jnp.concatenate([w_in[:, _DT_COL:_DT_COL + N_DIRS * SSM_HEADS], dt_pad], axis=1).astype(BF16),
        "qn": jnp.tile(p["q_norm"][i], A_HEADS).reshape(1, -1),
        "kn": jnp.tile(p["k_norm"][i], A_KV_HEADS).reshape(1, -1),
        "hm": jnp.asarray(hm, BF16),
        "n_pre": p["norm_mix_pre"][i].reshape(1, d),
        "n_post": p["norm_mix_post"][i].reshape(1, d),
        "n_ffn_pre": p["norm_ffn_pre"][i].reshape(1, d),
        "n_ffn_post": p["norm_ffn_post"][i].reshape(1, d),
        "bias_tab": _na_bias_table(p["na_rel_bias"][i]),
        "conv_w": jnp.pad(p["conv_w"][i], ((0, 8 - SSM_CONV), (0, 0))),
        "conv_b": p["conv_b"][i].reshape(1, -1),
        "dt_b": lane_pad(p["ssm_dt_bias"][i]),
        "dt_bt": p["ssm_dt_bias"][i].reshape(-1, 1),
        "a_row": lane_pad(a),
        "a_col": a.reshape(-1, 1),
        "d_skip": jnp.repeat(p["ssm_d"][i, 0] + p["ssm_d"][i, 1], SSM_HEAD_DIM).reshape(1, -1),
        "ssm_norm": p["ssm_norm"][i].reshape(1, -1),
        "w_a": p["w_branch_a"][i].astype(BF16),
        "w_b": p["w_branch_b"][i].astype(BF16),
        "w_c": p["w_branch_c"][i].astype(BF16),
        "w_out": p["w_out"][i].astype(BF16),
        "w_rt": p["w_router"][i].T,
        "layer": i,
        "w_gate": p["w_exp_gate"],
        "w_up": p["w_exp_up"],
        "w_down": p["w_exp_down"],
    }


def _ssd_branch(px, pc, lw, with_ctx_out):
    zx, xbcx, dtx = px
    zc, xbcc, dtc = pc
    b = zx.shape[0]
    nd = N_DIRS * SSM_HEADS
    dttx = dtx[:, :, :nd].transpose(0, 2, 1)
    dttc = dtc[:, :, :nd].transpose(0, 2, 1)
    h0 = jnp.zeros((b, SSM_HEADS, SSM_HEAD_DIM, SSM_STATE), F32)
    y0c, hc0 = _ssd_pass(0, xbcc, dtc, dttc, lw, h0)
    yc, hc1 = _ssd_pass(1, xbcc, dtc, dttc, lw, h0, fin=(y0c, zc))
    y0x, _ = _ssd_pass(0, xbcx, dtx, dttx, lw, hc0)
    yx, _ = _ssd_pass(1, xbcx, dtx, dttx, lw, hc1, fin=(y0x, zx))
    return yx, (yc if with_ctx_out else None)


def _layer(i, hx, hc, mod_x, mod_c, p, rope_tabs, last):
    lw = _layer_weights(i, p)
    d = hx.shape[2]
    sh_m, sc_m, g_m, sh_f, sc_f, g_f = [mod_x[:, None, j * d:(j + 1) * d] for j in range(N_ADA)]
    csh_m, csc_m, cg_m, csh_f, csc_f, cg_f = [mod_c[:, None, j * d:(j + 1) * d] for j in range(N_ADA)]

    qa_c, ka_c, va_c, qb_c, kb_c, vb_c, z_c, xbc_c, dt_c, gt_c = _proj(hc, lw["n_pre"], csh_m, csc_m, lw, None)
    qa, ka, va, qb, kb, vb, z, xbc, dt, gt = _proj(hx, lw["n_pre"], sh_m, sc_m, lw, rope_tabs)

    ya = _flash(qa, jnp.concatenate([ka, ka_c], axis=1), jnp.concatenate([va, va_c], axis=1),
                A_KV_HEADS, A_HEADS // A_KV_HEADS, True)
    yb = _na(qb, kb, vb, kb_c, vb_c, lw["bias_tab"])
    yc, yc_c = _ssd_branch((z, xbc, dt), (z_c, xbc_c, dt_c), lw, not last)
    hx, vx, aff = _merge(ya, yb, yc, gt, hx, lw, g_m, sh_f, sc_f)
    hx = _moe(vx, aff, hx, g_f, lw)
    if last:
        return hx, hc
    ya_c = _flash(qa_c, ka_c, va_c, A_KV_HEADS, A_HEADS // A_KV_HEADS, True)
    yb_c = _flash(qb_c, kb_c, vb_c, NA_HEADS, 1, False)
    hc, vc, aff_c = _merge(ya_c, yb_c, yc_c, gt_c, hc, lw, cg_m, csh_f, csc_f)
    hc = _moe(vc, aff_c, hc, cg_f, lw)
    return hx, hc


def kernel(x, c, ctx, c_ctx, w_ada, b_ada, norm_mix_pre, norm_mix_post, norm_ffn_pre, norm_ffn_post, w_in, q_norm, k_norm, na_rel_bias, conv_w, conv_b, ssm_a_log, ssm_dt_bias, ssm_d, ssm_norm, w_branch_a, w_branch_b, w_branch_c, w_out, w_router, w_exp_gate, w_exp_up, w_exp_down):
    p = dict(norm_mix_pre=norm_mix_pre, norm_mix_post=norm_mix_post, norm_ffn_pre=norm_ffn_pre,
             norm_ffn_post=norm_ffn_post, w_in=w_in, q_norm=q_norm, k_norm=k_norm, na_rel_bias=na_rel_bias,
             conv_w=conv_w, conv_b=conv_b, ssm_a_log=ssm_a_log, ssm_dt_bias=ssm_dt_bias, ssm_d=ssm_d,
             ssm_norm=ssm_norm, w_branch_a=w_branch_a, w_branch_b=w_branch_b, w_branch_c=w_branch_c,
             w_out=w_out, w_router=w_router, w_exp_gate=w_exp_gate, w_exp_up=w_exp_up, w_exp_down=w_exp_down)
    b, s, d = x.shape
    depth = w_ada.shape[0]
    assert b + 1 <= 8
    cc = jnp.concatenate([c, c_ctx[None, :], jnp.zeros((8 - b - 1, d), F32)], axis=0)
    mod = _ada_mod(cc, w_ada, b_ada)
    rope_tabs = _rope_tables(s)
    hx, hc = x, ctx
    for i in range(depth):
        mod_x = mod[i, :b]
        mod_c = jnp.broadcast_to(mod[i, b:b + 1], (b, mod.shape[2]))
        hx, hc = _layer(i, hx, hc, mod_x, mod_c, p, rope_tabs, i == depth - 1)
    return hx
```

```python
import functools

import jax
import jax.numpy as jnp
import numpy as np
from jax import lax
from jax.experimental import pallas as pl
from jax.experimental.pallas import tpu as pltpu

F32, BF16, I32 = jnp.float32, jnp.bfloat16, jnp.int32
HIGHEST = lax.Precision.HIGHEST

GRID_W = 64
HEAD_DIM = 64
ROPE_BASE = 10000.0
NORM_EPS = 1e-6
N_ADA = 6
A_HEADS, A_KV_HEADS = 8, 2
NA_HEADS, NA_KH, NA_KW = 8, 8, 16
SSM_HEADS, SSM_HEAD_DIM, SSM_INNER = 8, 64, 512
SSM_GROUPS, SSM_STATE, SSM_CONV, N_DIRS = 2, 64, 5, 2
XBC_DIM = SSM_INNER + 2 * SSM_GROUPS * SSM_STATE
A_Q_DIM, A_KV_DIM, NA_DIM = A_HEADS * HEAD_DIM, A_KV_HEADS * HEAD_DIM, NA_HEADS * HEAD_DIM
N_EXPERTS, EC_CAPACITY = 16, 2
N_BRANCH = 3
ATT_SCALE = HEAD_DIM ** -0.5
LOG2_E = 1.4426950408889634

V7X_LANES = 128
V7X_VMEM_LIMIT_BYTES = 56 * 1024 * 1024
NEG_BIG = -1e30

SSD_CHUNK = 256
ROUTE_TILE = 256


def _cparams(sem):
    return pltpu.CompilerParams(dimension_semantics=sem, vmem_limit_bytes=V7X_VMEM_LIMIT_BYTES)


def _nt_dot(a, b, precision=None):
    return lax.dot_general(a, b, (((1,), (1,)), ((), ())), preferred_element_type=F32,
                           precision=precision)


def _dot(a, b, precision=None):
    return jnp.dot(a, b, preferred_element_type=F32, precision=precision)


def _silu(x):
    return x * jax.nn.sigmoid(x)


def _rms(x, gain):
    return x * lax.rsqrt(jnp.mean(x * x, axis=-1, keepdims=True) + NORM_EPS) * gain


def _ada_kernel(c_ref, w_ref, b_ref, o_ref):
    o_ref[0] = _dot(_silu(c_ref[...]), w_ref[0], HIGHEST) + b_ref[0]


def _ada_mod(cc, w_ada, b_ada):
    nl, d, n = w_ada.shape
    tn = 512
    return pl.pallas_call(
        _ada_kernel,
        out_shape=jax.ShapeDtypeStruct((nl, 8, n), F32),
        grid=(nl, n // tn),
        in_specs=[pl.BlockSpec((8, d), lambda l, j: (0, 0)),
                  pl.BlockSpec((1, d, tn), lambda l, j: (l, 0, j)),
                  pl.BlockSpec((1, 1, tn), lambda l, j: (l, 0, j))],
        out_specs=pl.BlockSpec((1, 8, tn), lambda l, j: (l, 0, j)),
        compiler_params=_cparams(("parallel", "parallel")),
        name="ada_mod",
    )(cc, w_ada, b_ada.reshape(nl, 1, n))


_OFF_QA, _OFF_KA, _OFF_VA, _OFF_QB, _OFF_KB, _OFF_VB, _OFF_Z, _OFF_XBC, _OFF_G, _OFF_END = (
    0, 512, 640, 768, 1280, 1792, 2304, 2816, 3584, 6656)
_DT_COL = 3584


def _head_rms(x, hm, gain):
    ms = _dot((x * x).astype(BF16), hm)
    return x * lax.rsqrt(ms + NORM_EPS) * gain


def _rope(x, cos, sin):
    w = x.shape[-1]
    lane = lax.broadcasted_iota(I32, x.shape, 1)
    first = (lane & 16) == 0
    rot = jnp.where(first, -pltpu.roll(x, w - 16, 1), pltpu.roll(x, 16, 1))
    return x * cos + rot * sin


def _proj_kernel(rope, h_ref, g_ref, sh_ref, sc_ref, w_ref, wdt_ref, qn_ref, kn_ref, hm_ref, *rest):
    if rope:
        cos_ref, sin_ref = rest[:2]
        rest = rest[2:]
    qa_o, ka_o, va_o, qb_o, kb_o, vb_o, z_o, xbc_o, dt_o, gt_o = rest
    u = _rms(h_ref[0], g_ref[...]) * (1.0 + sc_ref[0]) + sh_ref[0]
    ub = u.astype(BF16)

    def mm(lo, hi):
        return _dot(ub, w_ref[:, lo:hi])

    qa = _head_rms(mm(_OFF_QA, _OFF_KA), hm_ref[...], qn_ref[...])
    ka = _head_rms(mm(_OFF_KA, _OFF_VA), hm_ref[:A_KV_DIM, :A_KV_DIM], kn_ref[...])
    if rope:
        cs, sn = cos_ref[...], sin_ref[...]
        qa = _rope(qa, jnp.concatenate([cs] * 4, axis=1), jnp.concatenate([sn] * 4, axis=1))
        ka = _rope(ka, cs, sn)
    qa_o[0] = (qa * (ATT_SCALE * LOG2_E)).astype(BF16)
    ka_o[0] = ka.astype(BF16)
    va_o[0] = mm(_OFF_VA, _OFF_QB).astype(BF16)
    qb_o[0] = (mm(_OFF_QB, _OFF_KB) * ATT_SCALE).astype(BF16)
    kb_o[0] = mm(_OFF_KB, _OFF_VB).astype(BF16)
    vb_o[0] = mm(_OFF_VB, _OFF_Z).astype(BF16)
    z_o[0] = mm(_OFF_Z, _OFF_XBC)
    xbc_o[0] = mm(_OFF_XBC, _OFF_G)
    dt_o[0] = _dot(ub, wdt_ref[...])
    for i in range((_OFF_END - _OFF_G) // 512):
        gt_o[0, :, i * 512:(i + 1) * 512] = mm(_OFF_G + i * 512, _OFF_G + (i + 1) * 512)


def _proj(h, gain, shift, scale, lw, rope_tabs):
    b, t, d = h.shape
    tm = 256
    rope = rope_tabs is not None
    const = lambda shape: pl.BlockSpec(shape, lambda i, j: (0,) * len(shape))
    tok = lambda w: pl.BlockSpec((1, tm, w), lambda i, j: (i, j, 0))
    vec = pl.BlockSpec((1, 1, d), lambda i, j: (i, 0, 0))
    in_specs = [tok(d), const((1, d)), vec, vec, const((d, _OFF_END)), const((d, V7X_LANES)),
                const((1, A_Q_DIM)), const((1, A_KV_DIM)), const((A_Q_DIM, A_Q_DIM))]
    args = [h, gain, shift, scale, lw["w_main"], lw["w_dt"], lw["qn"], lw["kn"], lw["hm"]]
    if rope:
        tab = pl.BlockSpec((tm, V7X_LANES), lambda i, j: (j, 0))
        in_specs += [tab, tab]
        args += list(rope_tabs)
    widths = [(A_Q_DIM, BF16), (A_KV_DIM, BF16), (A_KV_DIM, BF16), (NA_DIM, BF16), (NA_DIM, BF16),
              (NA_DIM, BF16), (SSM_INNER, F32), (XBC_DIM, F32), (V7X_LANES, F32),
              (_OFF_END - _OFF_G, F32)]
    return pl.pallas_call(
        functools.partial(_proj_kernel, rope),
        out_shape=[jax.ShapeDtypeStruct((b, t, w), dt) for w, dt in widths],
        grid=(b, t // tm),
        in_specs=in_specs,
        out_specs=[tok(w) for w, _ in widths],
        compiler_params=_cparams(("parallel", "parallel")),
        name="norm_mod_proj",
    )(*args)


def _flash_kernel(nkv, grp, base2, q_ref, kt_ref, v_ref, o_ref, q_s, m_s, acc_s):
    j = pl.program_id(2)
    tq = q_ref.shape[1]
    tk = kt_ref.shape[2]
    hd = HEAD_DIM
    ex = jnp.exp2 if base2 else jnp.exp

    @pl.when(j == 0)
    def _():
        qb = q_ref[0]
        for g in range(nkv):
            parts = [qb[:, (g * grp + i) * hd:(g * grp + i + 1) * hd] for i in range(grp)]
            q_s[g] = parts[0] if grp == 1 else jnp.concatenate(parts, axis=0)
        m_s[...] = jnp.full(m_s.shape, -jnp.inf, F32)
        acc_s[...] = jnp.zeros(acc_s.shape, F32)

    for g in range(nkv):
        s = _dot(q_s[g], kt_ref[0, g * hd:(g + 1) * hd, :])
        m_prev = m_s[g]
        m_new = jnp.maximum(m_prev, jnp.max(s, axis=-1, keepdims=True))
        alpha = ex(m_prev - m_new)
        m_s[g] = m_new
        p = jnp.concatenate(
            [ex(s[:, c * V7X_LANES:(c + 1) * V7X_LANES] - m_new).astype(BF16) for c in range(tk // V7X_LANES)],
            axis=1)
        acc_s[g] = alpha * acc_s[g] + _dot(p, v_ref[0, :, g * V7X_LANES:(g + 1) * V7X_LANES])

    @pl.when(j == pl.num_programs(2) - 1)
    def _():
        for g in range(nkv):
            acc = acc_s[g]
            o = acc / pltpu.roll(acc, hd, 1)
            for i in range(grp):
                c0 = (g * grp + i) * hd
                o_ref[0, :, c0:c0 + hd] = o[i * tq:(i + 1) * tq, :hd].astype(o_ref.dtype)


def _pick_tile(n, cands):
    for c in cands:
        if n % c == 0:
            return c
    raise ValueError(f"no tile for {n}")


def _flash(q, k, v, nkv, grp, base2):
    b, s, qd = q.shape
    sk = k.shape[1]
    kt = k.transpose(0, 2, 1)
    ones = jnp.ones((b, sk, nkv, V7X_LANES - HEAD_DIM), v.dtype)
    vaug = jnp.concatenate([v.reshape(b, sk, nkv, HEAD_DIM), ones], axis=-1).reshape(b, sk, nkv * V7X_LANES)
    tq = _pick_tile(s, (512, 256, 128))
    tk = _pick_tile(sk, (3328, 1280, 1024, 512, 256, 128))
    rows = grp * tq
    return pl.pallas_call(
        functools.partial(_flash_kernel, nkv, grp, base2),
        out_shape=jax.ShapeDtypeStruct((b, s, qd), BF16),
        grid=(b, s // tq, sk // tk),
        in_specs=[pl.BlockSpec((1, tq, qd), lambda i, m, j: (i, m, 0)),
                  pl.BlockSpec((1, nkv * HEAD_DIM, tk), lambda i, m, j: (i, 0, j)),
                  pl.BlockSpec((1, tk, nkv * V7X_LANES), lambda i, m, j: (i, j, 0))],
        out_specs=pl.BlockSpec((1, tq, qd), lambda i, m, j: (i, m, 0)),
        scratch_shapes=[pltpu.VMEM((nkv, rows, HEAD_DIM), BF16),
                        pltpu.VMEM((nkv, rows, V7X_LANES), F32),
                        pltpu.VMEM((nkv, rows, V7X_LANES), F32)],
        compiler_params=_cparams(("parallel", "parallel", "arbitrary")),
        name=f"flash_attn_{nkv}x{grp}",
    )(q, kt, vaug)


_NA_ROWS_PER_STEP = 8
_NA_BLOCK = _NA_ROWS_PER_STEP * GRID_W
_NA_BAND = NA_KH * GRID_W


def _na_kernel(rows, q_ref, kp_ref, kc_ref, kn_ref, vp_ref, vc_ref, vn_ref, kx_ref, vx_ref, bias_ref,
               o_ref, kcat_s, vcat_s):
    g = pl.program_id(1)
    nb = _NA_BLOCK
    for i, (kr, vr) in enumerate(((kp_ref, vp_ref), (kc_ref, vc_ref), (kn_ref, vn_ref))):
        kcat_s[i * nb:(i + 1) * nb, :] = kr[0]
        vcat_s[i * nb:(i + 1) * nb, :] = vr[0]
    n_ctx = kx_ref.shape[1]
    lanes = V7X_LANES
    head0 = lax.broadcasted_iota(I32, (GRID_W, lanes), 1) < HEAD_DIM
    ones_loc = jnp.ones((_NA_BAND, lanes), BF16)
    ones_ctx = jnp.ones((n_ctx, lanes), BF16)

    def body(j, carry):
        r = g * _NA_ROWS_PER_STEP + j
        r0 = jnp.clip(r - NA_KH // 2, 0, rows - NA_KH)
        off = pl.multiple_of((r0 - g * _NA_ROWS_PER_STEP + _NA_ROWS_PER_STEP) * GRID_W, GRID_W)
        case = r - r0
        qrows = pl.ds(pl.multiple_of(j * GRID_W, GRID_W), GRID_W)
        for pr in range(NA_HEADS // 2):
            ls = slice(pr * lanes, (pr + 1) * lanes)
            qp = q_ref[0, qrows, ls]
            zero = jnp.zeros_like(qp)
            q2 = jnp.concatenate([jnp.where(head0, qp, zero), jnp.where(head0, zero, qp)], axis=0)
            kband = kcat_s[pl.ds(off, _NA_BAND), ls]
            vband = vcat_s[pl.ds(off, _NA_BAND), ls]
            s_loc = _nt_dot(q2, kband) + bias_ref[case * (NA_HEADS // 2) + pr]
            s_ctx = _nt_dot(q2, kx_ref[0, :, ls])
            m = jnp.maximum(jnp.max(s_loc, axis=-1, keepdims=True), jnp.max(s_ctx, axis=-1, keepdims=True))
            p_loc = jnp.exp(s_loc - m).astype(BF16)
            p_ctx = jnp.exp(s_ctx - m).astype(BF16)
            pvl = (_dot(p_loc, jnp.concatenate([vband, ones_loc], axis=1))
                   + _dot(p_ctx, jnp.concatenate([vx_ref[0, :, ls], ones_ctx], axis=1)))
            o2 = pvl[:, :lanes] / pvl[:, lanes:]
            o_ref[0, qrows, ls] = jnp.where(head0, o2[:GRID_W], o2[GRID_W:]).astype(o_ref.dtype)
        return carry

    lax.fori_loop(0, _NA_ROWS_PER_STEP, body, 0, unroll=4)


def _na_bias_table(rel_bias):
    nh, nr, nc = rel_bias.shape
    col = np.arange(GRID_W)
    c0 = np.clip(col - NA_KW // 2, 0, GRID_W - NA_KW)
    valid = (col[None, :] >= c0[:, None]) & (col[None, :] < c0[:, None] + NA_KW)
    w2 = 2 * GRID_W
    lo = GRID_W - NA_KW
    padded = jnp.pad(rel_bias.astype(F32), ((0, 0), (0, 0), (lo, w2 - lo - nc)))
    flat = jnp.broadcast_to(padded[:, :, None, :], (nh, nr, GRID_W, w2)).reshape(nh, nr, GRID_W * w2)
    skew = flat[:, :, :GRID_W * (w2 - 1)].reshape(nh, nr, GRID_W, w2 - 1)
    toe = skew[:, :, :, GRID_W - 1:]
    tab = jnp.stack([toe[:, NA_KH - 1 - case:2 * NA_KH - 1 - case] for case in range(NA_KH)])
    tab = jnp.where(valid[None, None, None], tab, NEG_BIG)
    tab = tab.transpose(0, 1, 3, 2, 4)
    return tab.reshape(NA_KH * NA_HEADS // 2, 2 * GRID_W, NA_KH * GRID_W)


def _na(q, k, v, kx, vx, bias_tab):
    b, s, d = q.shape
    l = kx.shape[1]
    rows = s // GRID_W
    assert rows % _NA_ROWS_PER_STEP == 0 and rows >= NA_KH
    ng = rows // _NA_ROWS_PER_STEP
    nb = _NA_BLOCK
    cur = pl.BlockSpec((1, nb, d), lambda i, g: (i, g, 0))
    prv = pl.BlockSpec((1, nb, d), lambda i, g: (i, jnp.maximum(g - 1, 0), 0))
    nxt = pl.BlockSpec((1, nb, d), lambda i, g: (i, jnp.minimum(g + 1, ng - 1), 0))
    ctx = pl.BlockSpec((1, l, d), lambda i, g: (i, 0, 0))
    return pl.pallas_call(
        functools.partial(_na_kernel, rows),
        out_shape=jax.ShapeDtypeStruct((b, s, d), BF16),
        grid=(b, ng),
        in_specs=[cur, prv, cur, nxt, prv, cur, nxt, ctx, ctx,
                  pl.BlockSpec(bias_tab.shape, lambda i, g: (0, 0, 0))],
        out_specs=cur,
        scratch_shapes=[pltpu.VMEM((3 * nb, d), BF16), pltpu.VMEM((3 * nb, d), BF16)],
        compiler_params=_cparams(("parallel", "parallel")),
        name="neighbourhood_attn",
    )(q, k, k, k, v, v, v, kx, vx, bias_tab)


def _softplus(x):
    return jnp.maximum(x, 0.0) + jnp.log1p(jnp.exp(-jnp.abs(x)))


def _ssd_kernel(d, final, xc_ref, xp_ref, xn_ref, dt_ref, dtt_ref, cw_ref, cb_ref, dtb_ref, dtbt_ref,
                a_ref, at_ref, h0_ref, *rest):
    if final:
        y0_ref, z_ref, dsk_ref, ng_ref, y_o, h_o, h_s, xcat_s = rest
    else:
        y0_ref = z_ref = dsk_ref = ng_ref = None
        y_o, h_o, h_s, xcat_s = rest
    t = pl.program_id(0)
    nc = pl.num_programs(0)
    reverse = d == 1
    c = nc - 1 - t if reverse else t
    q = xc_ref.shape[1]

    @pl.when(t == 0)
    def _():
        h_s[...] = h0_ref[...]

    pm = (c > 0).astype(F32)
    nm = (c < nc - 1).astype(F32)
    ii = lax.broadcasted_iota(I32, (q, q), 0)
    jj = lax.broadcasted_iota(I32, (q, q), 1)
    keep = (jj >= ii) if reverse else (jj <= ii)
    tri = keep.astype(F32)
    chan = lax.broadcasted_iota(I32, (V7X_LANES, SSM_INNER), 1)
    dt_col = lax.broadcasted_iota(I32, (V7X_LANES, SSM_INNER), 0)
    expand = (dt_col == d * SSM_HEADS + lax.shift_right_logical(chan, 6)).astype(F32)
    for bi in range(xc_ref.shape[0]):
        _ssd_chunk(d, bi, pm, nm, keep, tri, expand, xc_ref, xp_ref, xn_ref, dt_ref, dtt_ref, cw_ref, cb_ref,
                   dtb_ref, dtbt_ref, a_ref, at_ref, y0_ref, z_ref, dsk_ref, ng_ref, y_o, h_s, xcat_s)

    @pl.when(t == nc - 1)
    def _():
        h_o[...] = h_s[...]


def _ssd_chunk(d, bi, pm, nm, keep, tri, expand, xc_ref, xp_ref, xn_ref, dt_ref, dtt_ref, cw_ref, cb_ref,
               dtb_ref, dtbt_ref, a_ref, at_ref, y0_ref, z_ref, dsk_ref, ng_ref, y_o, h_s, xcat_s):
    final = y0_ref is not None
    reverse = d == 1
    q = xc_ref.shape[1]
    hp = SSM_HEAD_DIM
    ns = SSM_STATE

    xcat_s[bi, 0:8, :] = xp_ref[bi] * pm
    xcat_s[bi, 8:8 + q, :] = xc_ref[bi]
    xcat_s[bi, 8 + q:16 + q, :] = xn_ref[bi] * nm
    acc = jnp.zeros((q, XBC_DIM), F32)
    for k in range(SSM_CONV):
        acc = acc + xcat_s[bi, 8 - SSM_CONV // 2 + k:8 - SSM_CONV // 2 + k + q, :] * cw_ref[k:k + 1, :]
    u = _silu(acc + cb_ref[...])
    xs = u[:, :SSM_INNER]
    bm = u[:, SSM_INNER:SSM_INNER + SSM_GROUPS * ns]
    cm = u[:, SSM_INNER + SSM_GROUPS * ns:]

    dt = _softplus(dt_ref[bi] + dtb_ref[...])
    dtt = _softplus(dtt_ref[bi] + dtbt_ref[...])
    acs = _dot(tri, dt * a_ref[...], HIGHEST)
    acst = _nt_dot(dtt * at_ref[...], tri, HIGHEST)
    last = 0 if reverse else q - 1

    xdt_all = xs * _dot(dt, expand, HIGHEST)
    xdt_t = [xdt_all[:, p * V7X_LANES:(p + 1) * V7X_LANES].T for p in range(SSM_INNER // V7X_LANES)]
    xdt_b = xdt_all.astype(BF16)
    gmats = []
    for gi in range(SSM_GROUPS):
        gmats.append(_nt_dot(cm[:, gi * ns:(gi + 1) * ns].astype(BF16),
                             bm[:, gi * ns:(gi + 1) * ns].astype(BF16)))
    ys = []
    for h in range(SSM_HEADS):
        col = d * SSM_HEADS + h
        gi = h // (SSM_HEADS // SSM_GROUPS)
        a_col = acs[:, col:col + 1]
        a_row = acst[col:col + 1, :]
        tot = a_col[last:last + 1, :]
        lmat = jnp.exp(jnp.where(keep, a_col - a_row, NEG_BIG))
        y_diag = _dot((gmats[gi] * lmat).astype(BF16), xdt_b[:, h * hp:(h + 1) * hp])
        hst = h_s[bi, h]
        cg = cm[:, gi * ns:(gi + 1) * ns]
        bg = bm[:, gi * ns:(gi + 1) * ns]
        y_off = _nt_dot((cg * jnp.exp(a_col)).astype(BF16), hst.astype(BF16))
        ys.append(y_diag + y_off)
        bdec = (bg * jnp.exp(tot - a_col)).astype(BF16)
        xt = xdt_t[h // 2][(h % 2) * hp:(h % 2 + 1) * hp, :]
        h_s[bi, h] = jnp.exp(tot) * hst + _dot(xt.astype(BF16), bdec)
    y = jnp.concatenate(ys, axis=1)

    if final:
        yt = y0_ref[bi] + y + dsk_ref[...] * xs
        y_o[bi] = _rms(yt * _silu(z_ref[bi]), ng_ref[...]).astype(y_o.dtype)
    else:
        y_o[bi] = y


def _ssd_pass(d, xbc, dt, dtt, lw, h0, fin=None):
    b, l, _ = xbc.shape
    q = SSD_CHUNK
    nc = l // q
    final = fin is not None
    cidx = (lambda t: nc - 1 - t) if d == 1 else (lambda t: t)
    r8 = q // 8
    nb8 = l // 8
    const = lambda shape: pl.BlockSpec(shape, lambda t: (0,) * len(shape))
    tok = lambda w: pl.BlockSpec((b, q, w), lambda t: (0, cidx(t), 0))
    hspec = pl.BlockSpec((b, SSM_HEADS, SSM_HEAD_DIM, SSM_STATE), lambda t: (0, 0, 0, 0))
    in_specs = [
        tok(XBC_DIM),
        pl.BlockSpec((b, 8, XBC_DIM), lambda t: (0, jnp.maximum(cidx(t) * r8 - 1, 0), 0)),
        pl.BlockSpec((b, 8, XBC_DIM), lambda t: (0, jnp.minimum((cidx(t) + 1) * r8, nb8 - 1), 0)),
        tok(V7X_LANES),
        pl.BlockSpec((b, 16, q), lambda t: (0, 0, cidx(t))),
        const((8, XBC_DIM)), const((1, XBC_DIM)), const((1, V7X_LANES)), const((16, 1)),
        const((1, V7X_LANES)), const((16, 1)), hspec]
    args = [xbc, xbc, xbc, dt, dtt, lw["conv_w"], lw["conv_b"], lw["dt_b"], lw["dt_bt"],
            lw["a_row"], lw["a_col"], h0]
    if final:
        in_specs += [tok(SSM_INNER), tok(SSM_INNER), const((1, SSM_INNER)), const((1, SSM_INNER))]
        args += [fin[0], fin[1], lw["d_skip"], lw["ssm_norm"]]
    return pl.pallas_call(
        functools.partial(_ssd_kernel, d, final),
        out_shape=[jax.ShapeDtypeStruct((b, l, SSM_INNER), BF16 if final else F32),
                   jax.ShapeDtypeStruct(h0.shape, F32)],
        grid=(nc,),
        in_specs=in_specs,
        out_specs=[tok(SSM_INNER), hspec],
        scratch_shapes=[pltpu.VMEM((b, SSM_HEADS, SSM_HEAD_DIM, SSM_STATE), F32),
                        pltpu.VMEM((b, q + 16, XBC_DIM), F32)],
        compiler_params=_cparams(("arbitrary",)),
        name=f"ssd_scan_d{d}{'_final' if final else ''}",
    )(*args)


def _merge_kernel(ya_ref, yb_ref, yc_ref, gt_ref, hx_ref, wa_ref, wb_ref, wc_ref, wo_ref, gm_ref,
                  npost_ref, npre_ref, shf_ref, scf_ref, wr_ref, hx_o, vx_o, aff_o):
    d = hx_ref.shape[2]
    gt = gt_ref[0]
    m = (jax.nn.sigmoid(gt[:, :d]) * _dot(ya_ref[0], wa_ref[...])
         + jax.nn.sigmoid(gt[:, d:2 * d]) * _dot(yb_ref[0], wb_ref[...])
         + jax.nn.sigmoid(gt[:, 2 * d:]) * _dot(yc_ref[0], wc_ref[...]))
    out = _dot(m.astype(BF16), wo_ref[...])
    hx = hx_ref[0] + gm_ref[0] * _rms(out, npost_ref[...])
    hx_o[0] = hx
    v = _rms(hx, npre_ref[...]) * (1.0 + scf_ref[0]) + shf_ref[0]
    vx_o[0] = v.astype(BF16)
    logits = _nt_dot(wr_ref[...], v, HIGHEST)
    e = jnp.exp(logits - jnp.max(logits, axis=0, keepdims=True))
    aff_o[0] = e / jnp.sum(e, axis=0, keepdims=True)


def _merge(ya, yb, yc, gates, hx, lw, gm, shf, scf):
    b, t, d = hx.shape
    tm = _pick_tile(t, (512, 256))
    const = lambda shape: pl.BlockSpec(shape, lambda i, j: (0,) * len(shape))
    tok = lambda w: pl.BlockSpec((1, tm, w), lambda i, j: (i, j, 0))
    vec = pl.BlockSpec((1, 1, d), lambda i, j: (i, 0, 0))
    return pl.pallas_call(
        _merge_kernel,
        out_shape=[jax.ShapeDtypeStruct((b, t, d), F32), jax.ShapeDtypeStruct((b, t, d), BF16),
                   jax.ShapeDtypeStruct((b, N_EXPERTS, t), F32)],
        grid=(b, t // tm),
        in_specs=[tok(A_Q_DIM), tok(NA_DIM), tok(SSM_INNER), tok(N_BRANCH * d), tok(d),
                  const((A_Q_DIM, d)), const((NA_DIM, d)), const((SSM_INNER, d)), const((d, d)),
                  vec, const((1, d)), const((1, d)), vec, vec, const((N_EXPERTS, d))],
        out_specs=[tok(d), tok(d), pl.BlockSpec((1, N_EXPERTS, tm), lambda i, j: (i, 0, j))],
        compiler_params=_cparams(("parallel", "parallel")),
        name="merge_residual_router",
    )(ya, yb, yc, gates, hx, lw["w_a"], lw["w_b"], lw["w_c"], lw["w_out"], gm, lw["n_post"],
      lw["n_ffn_pre"], shf, scf, lw["w_rt"])


def _topk_kernel(cap, aff_ref, pos_o, gate_o, cs_o, eq_s, gt_s, ceq_s, csel_s):
    x = aff_ref[0]
    ne, t = x.shape
    bits = pltpu.bitcast(x, I32)
    theta = jnp.zeros((ne, 1), I32)
    for bit in range(30, -1, -1):
        cand = theta | (1 << bit)
        cnt = jnp.sum((bits >= cand).astype(I32), axis=-1, keepdims=True)
        theta = jnp.where(cnt >= cap, cand, theta)
    gt = bits > theta
    eq = bits == theta
    need = cap - jnp.sum(gt.astype(I32), axis=-1, keepdims=True)
    eq_s[...] = eq.astype(F32)
    gt_s[...] = gt.astype(F32)
    tb = ROUTE_TILE
    ii = lax.broadcasted_iota(I32, (tb, tb), 0)
    jj = lax.broadcasted_iota(I32, (tb, tb), 1)
    upper = (ii <= jj).astype(BF16)
    lane = lax.broadcasted_iota(I32, (ne, V7X_LANES), 1)
    needf = need.astype(F32)

    ceq_s[...] = jnp.zeros(ceq_s.shape, F32)
    csel_s[...] = jnp.zeros(csel_s.shape, F32)
    cs_o[0] = jnp.zeros((ne, V7X_LANES), I32)

    def body(k, carry):
        c_eq, c_sel = ceq_s[...], csel_s[...]
        sl = pl.ds(pl.multiple_of(k * tb, tb), tb)
        e_k = eq_s[:, sl]
        g_k = gt_s[:, sl]
        inc_eq = _dot(e_k.astype(BF16), upper)
        rank_eq = c_eq + inc_eq - e_k
        sel = g_k + e_k * (rank_eq < needf).astype(F32)
        inc_sel = _dot(sel.astype(BF16), upper)
        slot = c_sel + inc_sel - sel
        pos_o[0, :, sl] = jnp.where(sel > 0.5, slot, -1.0).astype(I32)
        gate_o[0, :, sl] = jnp.where(sel > 0.5, aff_ref[0, :, sl], 0.0)
        cs_o[0] = jnp.where(lane == k, c_sel.astype(I32), cs_o[0])
        ceq_s[...] = c_eq + inc_eq[:, tb - 1:tb]
        csel_s[...] = c_sel + inc_sel[:, tb - 1:tb]
        return carry

    nk = t // tb
    lax.fori_loop(0, nk, body, 0)
    cs_o[0] = jnp.where(lane == nk, csel_s[...].astype(I32), cs_o[0])


def _topk(aff_t, cap):
    b, ne, t = aff_t.shape
    assert t % ROUTE_TILE == 0 and t // ROUTE_TILE < V7X_LANES
    full = pl.BlockSpec((1, ne, t), lambda i: (i, 0, 0))
    return pl.pallas_call(
        functools.partial(_topk_kernel, cap),
        out_shape=[jax.ShapeDtypeStruct((b, ne, t), I32), jax.ShapeDtypeStruct((b, ne, t), F32),
                   jax.ShapeDtypeStruct((b, ne, V7X_LANES), I32)],
        grid=(b,),
        in_specs=[full],
        out_specs=[full, full, pl.BlockSpec((1, ne, V7X_LANES), lambda i: (i, 0, 0))],
        scratch_shapes=[pltpu.VMEM((ne, t), F32), pltpu.VMEM((ne, t), F32),
                        pltpu.VMEM((ne, 1), F32), pltpu.VMEM((ne, 1), F32)],
        compiler_params=_cparams(("parallel",)),
        name="expert_topk",
    )(aff_t)


def _cs_at(cs_ref, b, e, k):
    return cs_ref[(b * N_EXPERTS + e) * V7X_LANES + k]


_GATHER_SMALL = 64


_GATHER_TILES = 8


def _gather_kernel(cap, cs_ref, pos_ref, vx_ref, xs_o):
    b, e, g = pl.program_id(0), pl.program_id(1), pl.program_id(2)
    tt = ROUTE_TILE
    nt = vx_ref.shape[1] // tt
    narrow = min(2 * _GATHER_SMALL, cap)
    wide = min(_GATHER_SMALL + tt, cap)

    @pl.when(g == 0)
    def _():
        xs_o[...] = jnp.zeros(xs_o.shape, xs_o.dtype)

    def tile(i, carry):
        k = g * nt + i
        c0 = _cs_at(cs_ref, b, e, k)
        n_sel = _cs_at(cs_ref, b, e, k + 1) - c0
        toks = pl.ds(pl.multiple_of(i * tt, tt), tt)
        slot = pos_ref[0, 0, :, toks]

        def place(wn):
            w0 = jnp.minimum((c0 // _GATHER_SMALL) * _GATHER_SMALL, cap - wn)
            w0 = pl.multiple_of(w0, min(_GATHER_SMALL, cap))
            onehot = (lax.broadcasted_iota(I32, (wn, tt), 0) + w0 == slot).astype(BF16)
            rows = _dot(onehot, vx_ref[0, toks, :])
            win = xs_o[0, 0, pl.ds(w0, wn), :].astype(F32)
            xs_o[0, 0, pl.ds(w0, wn), :] = (win + rows).astype(xs_o.dtype)

        if narrow < wide:
            pl.when(jnp.logical_and(n_sel > 0, n_sel <= _GATHER_SMALL))(lambda: place(narrow))
            pl.when(n_sel > _GATHER_SMALL)(lambda: place(wide))
        else:
            pl.when(n_sel > 0)(lambda: place(wide))
        return carry

    lax.fori_loop(0, nt, tile, 0)


def _gather(cs_flat, pos_t, vx, cap):
    b, t, d = vx.shape
    ne = N_EXPERTS
    nt = min(_GATHER_TILES, t // ROUTE_TILE)
    tg = nt * ROUTE_TILE
    assert t % tg == 0 and (cap % _GATHER_SMALL == 0 or cap < _GATHER_SMALL)
    grid_spec = pltpu.PrefetchScalarGridSpec(
        num_scalar_prefetch=1,
        grid=(b, ne, t // tg),
        in_specs=[pl.BlockSpec((1, 1, 1, tg), lambda i, e, g, cs: (i, e, 0, g)),
                  pl.BlockSpec((1, tg, d), lambda i, e, g, cs: (i, g, 0))],
        out_specs=pl.BlockSpec((1, 1, cap, d), lambda i, e, g, cs: (i, e, 0, 0)))
    return pl.pallas_call(
        functools.partial(_gather_kernel, cap),
        out_shape=jax.ShapeDtypeStruct((b, ne, cap, d), BF16),
        grid_spec=grid_spec,
        compiler_params=_cparams(("parallel", "parallel", "arbitrary")),
        name="expert_gather",
    )(cs_flat, pos_t.reshape(b, ne, 1, t), vx)


_FFN_TILE_F = 512


def _ffn_kernel(x_ref, wg_ref, wu_ref, wd_ref, y_o, acc_s):
    j = pl.program_id(2)
    nb, _, cap, d = x_ref.shape
    x = x_ref[...].reshape(nb * cap, d)
    hid = _silu(_dot(x, wg_ref[0, 0].astype(BF16))) * _dot(x, wu_ref[0, 0].astype(BF16))

    @pl.when(j == 0)
    def _():
        acc_s[...] = jnp.zeros(acc_s.shape, F32)

    acc_s[...] += _dot(hid.astype(BF16), wd_ref[0, 0].astype(BF16))

    @pl.when(j == pl.num_programs(2) - 1)
    def _():
        y_o[...] = acc_s[...].astype(y_o.dtype).reshape(y_o.shape)


_FFN_MAX_ROWS = 2048


def _expert_ffn(xs, layer, wg, wu, wd):
    b, ne, cap, d = xs.shape
    f = wg.shape[3]
    tf = _FFN_TILE_F
    nb = b if b * cap <= _FFN_MAX_ROWS else 1
    tok = pl.BlockSpec((nb, 1, cap, d), lambda e, i, j: (i, e, 0, 0))
    return pl.pallas_call(
        _ffn_kernel,
        out_shape=jax.ShapeDtypeStruct(xs.shape, BF16),
        grid=(ne, b // nb, f // tf),
        in_specs=[tok,
                  pl.BlockSpec((1, 1, d, tf), lambda e, i, j: (layer, e, 0, j)),
                  pl.BlockSpec((1, 1, d, tf), lambda e, i, j: (layer, e, 0, j)),
                  pl.BlockSpec((1, 1, tf, d), lambda e, i, j: (layer, e, j, 0))],
        out_specs=tok,
        scratch_shapes=[pltpu.VMEM((nb * cap, d), F32)],
        compiler_params=_cparams(("parallel", "parallel", "arbitrary")),
        name="expert_ffn",
    )(xs, wg, wu, wd)


_SLOT_ALIGN = 16


def _win_start(c0, cap, win):
    return jnp.minimum((c0 // _SLOT_ALIGN) * _SLOT_ALIGN, cap - win)


def _combine_kernel(cap, win, win_fast, cs_ref, *refs):
    ne = N_EXPERTS
    ys = refs[:ne]
    pos_ref, gate_ref, hx_ref, gf_ref, npost_ref, hx_o = refs[ne:]
    b, k = pl.program_id(0), pl.program_id(1)
    tt, d = hx_ref.shape[1], hx_ref.shape[2]
    pos, gate = pos_ref[0], gate_ref[0]
    starts, fits = [], None
    for e in range(ne):
        c0 = _cs_at(cs_ref, b, e, k)
        a = _win_start(c0, cap, win)
        starts.append(a)
        fit = _cs_at(cs_ref, b, e, k + 1) - a <= win_fast
        fits = fit if fits is None else jnp.logical_and(fits, fit)

    def run(width):
        lane = lax.broadcasted_iota(I32, (tt, width), 1)
        acc = jnp.zeros((tt, d), F32)
        for e in range(ne):
            local = pos[:, e:e + 1] - starts[e]
            onehot = jnp.where(lane == local, gate[:, e:e + 1], 0.0).astype(BF16)
            acc = acc + _dot(onehot, ys[e][0, 0, :width, :])
        hx_o[0] = hx_ref[0] + gf_ref[0] * _rms(acc, npost_ref[...])

    if win_fast < win:
        pl.when(fits)(lambda: run(win_fast))
        pl.when(jnp.logical_not(fits))(lambda: run(win))
    else:
        run(win)


def _combine(cs_flat, ys, pos_r, gate_r, hx, gf, n_post):
    b, t, d = hx.shape
    ne, cap = ys.shape[1], ys.shape[2]
    tt = ROUTE_TILE
    win = min(tt + _SLOT_ALIGN, cap)
    win_fast = min(tt, win)
    assert (cap - win) % _SLOT_ALIGN == 0

    def window(e):
        return pl.BlockSpec(
            (pl.Element(1), pl.Element(1), pl.Element(win), pl.Element(d)),
            lambda i, k, cs: (i, e, pl.multiple_of(_win_start(cs[(i * ne + e) * V7X_LANES + k], cap, win), _SLOT_ALIGN), 0))

    tok = lambda w: pl.BlockSpec((1, tt, w), lambda i, k, cs: (i, k, 0))
    grid_spec = pltpu.PrefetchScalarGridSpec(
        num_scalar_prefetch=1,
        grid=(b, t // tt),
        in_specs=[window(e) for e in range(ne)] + [
            tok(ne), tok(ne), tok(d),
            pl.BlockSpec((1, 1, d), lambda i, k, cs: (i, 0, 0)),
            pl.BlockSpec((1, d), lambda i, k, cs: (0, 0))],
        out_specs=tok(d))
    return pl.pallas_call(
        functools.partial(_combine_kernel, cap, win, win_fast),
        out_shape=jax.ShapeDtypeStruct((b, t, d), F32),
        grid_spec=grid_spec,
        compiler_params=_cparams(("parallel", "parallel")),
        name="expert_combine",
    )(cs_flat, *([ys] * ne), pos_r, gate_r, hx, gf, n_post)


def _moe(vx, aff_t, hx, gf, lw):
    b, t, d = hx.shape
    cap = EC_CAPACITY * t // N_EXPERTS
    pos_t, gate_t, cs = _topk(aff_t, cap)
    cs_flat = cs.reshape(-1)
    xs = _gather(cs_flat, pos_t, vx, cap)
    ys = _expert_ffn(xs, lw["layer"], lw["w_gate"], lw["w_up"], lw["w_down"])
    return _combine(cs_flat, ys, pos_t.transpose(0, 2, 1), gate_t.transpose(0, 2, 1), hx, gf,
                    lw["n_ffn_post"])


def _rope_tables(s):
    t = jnp.arange(s, dtype=I32)
    rows = (t // GRID_W).astype(F32)
    cols = (t % GRID_W).astype(F32)
    n_freq = HEAD_DIM // 4
    inv = ROPE_BASE ** (-jnp.arange(n_freq, dtype=F32) / n_freq)
    ar = rows[:, None] * inv[None, :]
    ac = cols[:, None] * inv[None, :]
    ang = jnp.concatenate([ar, ar, ac, ac] * (V7X_LANES // HEAD_DIM), axis=-1)
    return jnp.cos(ang), jnp.sin(ang)


def _layer_weights(i, p):
    d = p["w_in"].shape[1]
    w_in = p["w_in"][i]
    dt_pad = jnp.zeros((d, V7X_LANES - N_DIRS * SSM_HEADS), F32)
    hm = np.kron(np.eye(A_HEADS, dtype=np.float32), np.full((HEAD_DIM, HEAD_DIM), 1.0 / HEAD_DIM, np.float32))
    lane_pad = lambda v: jnp.pad(v.reshape(1, -1), ((0, 0), (0, V7X_LANES - v.size)))
    a = -jnp.exp(p["ssm_a_log"][i].astype(F32)).reshape(-1)
    return {
        "w_main": jnp.concatenate([w_in[:, :_DT_COL], w_in[:, _DT_COL + N_DIRS * SSM_HEADS:]], axis=1).astype(BF16),
        "w_dt": jnp.concatenate([w_in[:, _DT_COL:_DT_COL + N_DIRS * SSM_HEADS], dt_pad], axis=1).astype(BF16),
        "qn": jnp.tile(p["q_norm"][i], A_HEADS).reshape(1, -1),
        "kn": jnp.tile(p["k_norm"][i], A_KV_HEADS).reshape(1, -1),
        "hm": jnp.asarray(hm, BF16),
        "n_pre": p["norm_mix_pre"][i].reshape(1, d),
        "n_post": p["norm_mix_post"][i].reshape(1, d),
        "n_ffn_pre": p["norm_ffn_pre"][i].reshape(1, d),
        "n_ffn_post": p["norm_ffn_post"][i].reshape(1, d),
        "bias_tab": _na_bias_table(p["na_rel_bias"][i]),
        "conv_w": jnp.pad(p["conv_w"][i], ((0, 8 - SSM_CONV), (0, 0))),
        "conv_b": p["conv_b"][i].reshape(1, -1),
        "dt_b": lane_pad(p["ssm_dt_bias"][i]),
        "dt_bt": p["ssm_dt_bias"][i].reshape(-1, 1),
        "a_row": lane_pad(a),
        "a_col": a.reshape(-1, 1),
        "d_skip": jnp.repeat(p["ssm_d"][i, 0] + p["ssm_d"][i, 1], SSM_HEAD_DIM).reshape(1, -1),
        "ssm_norm": p["ssm_norm"][i].reshape(1, -1),
        "w_a": p["w_branch_a"][i].astype(BF16),
        "w_b": p["w_branch_b"][i].astype(BF16),
        "w_c": p["w_branch_c"][i].astype(BF16),
        "w_out": p["w_out"][i].astype(BF16),
        "w_rt": p["w_router"][i].T,
        "layer": i,
        "w_gate": p["w_exp_gate"],
        "w_up": p["w_exp_up"],
        "w_down": p["w_exp_down"],
    }


def _ssd_branch(px, pc, lw, with_ctx_out):
    zx, xbcx, dtx = px
    zc, xbcc, dtc = pc
    b = zx.shape[0]
    nd = N_DIRS * SSM_HEADS
    dttx = dtx[:, :, :nd].transpose(0, 2, 1)
    dttc = dtc[:, :, :nd].transpose(0, 2, 1)
    h0 = jnp.zeros((b, SSM_HEADS, SSM_HEAD_DIM, SSM_STATE), F32)
    y0c, hc0 = _ssd_pass(0, xbcc, dtc, dttc, lw, h0)
    yc, hc1 = _ssd_pass(1, xbcc, dtc, dttc, lw, h0, fin=(y0c, zc))
    y0x, _ = _ssd_pass(0, xbcx, dtx, dttx, lw, hc0)
    yx, _ = _ssd_pass(1, xbcx, dtx, dttx, lw, hc1, fin=(y0x, zx))
    return yx, (yc if with_ctx_out else None)


def _layer(i, hx, hc, mod_x, mod_c, p, rope_tabs, last):
    lw = _layer_weights(i, p)
    d = hx.shape[2]
    sh_m, sc_m, g_m, sh_f, sc_f, g_f = [mod_x[:, None, j * d:(j + 1) * d] for j in range(N_ADA)]
    csh_m, csc_m, cg_m, csh_f, csc_f, cg_f = [mod_c[:, None, j * d:(j + 1) * d] for j in range(N_ADA)]

    qa_c, ka_c, va_c, qb_c, kb_c, vb_c, z_c, xbc_c, dt_c, gt_c = _proj(hc, lw["n_pre"], csh_m, csc_m, lw, None)
    qa, ka, va, qb, kb, vb, z, xbc, dt, gt = _proj(hx, lw["n_pre"], sh_m, sc_m, lw, rope_tabs)

    ya = _flash(qa, jnp.concatenate([ka, ka_c], axis=1), jnp.concatenate([va, va_c], axis=1),
                A_KV_HEADS, A_HEADS // A_KV_HEADS, True)
    yb = _na(qb, kb, vb, kb_c, vb_c, lw["bias_tab"])
    yc, yc_c = _ssd_branch((z, xbc, dt), (z_c, xbc_c, dt_c), lw, not last)
    hx, vx, aff = _merge(ya, yb, yc, gt, hx, lw, g_m, sh_f, sc_f)
    hx = _moe(vx, aff, hx, g_f, lw)
    if last:
        return hx, hc
    ya_c = _flash(qa_c, ka_c, va_c, A_KV_HEADS, A_HEADS // A_KV_HEADS, True)
    yb_c = _flash(qb_c, kb_c, vb_c, NA_HEADS, 1, False)
    hc, vc, aff_c = _merge(ya_c, yb_c, yc_c, gt_c, hc, lw, cg_m, csh_f, csc_f)
    hc = _moe(vc, aff_c, hc, cg_f, lw)
    return hx, hc


def kernel(x, c, ctx, c_ctx, w_ada, b_ada, norm_mix_pre, norm_mix_post, norm_ffn_pre, norm_ffn_post, w_in, q_norm, k_norm, na_rel_bias, conv_w, conv_b, ssm_a_log, ssm_dt_bias, ssm_d, ssm_norm, w_branch_a, w_branch_b, w_branch_c, w_out, w_router, w_exp_gate, w_exp_up, w_exp_down):
    p = dict(norm_mix_pre=norm_mix_pre, norm_mix_post=norm_mix_post, norm_ffn_pre=norm_ffn_pre,
             norm_ffn_post=norm_ffn_post, w_in=w_in, q_norm=q_norm, k_norm=k_norm, na_rel_bias=na_rel_bias,
             conv_w=conv_w, conv_b=conv_b, ssm_a_log=ssm_a_log, ssm_dt_bias=ssm_dt_bias, ssm_d=ssm_d,
             ssm_norm=ssm_norm, w_branch_a=w_branch_a, w_branch_b=w_branch_b, w_branch_c=w_branch_c,
             w_out=w_out, w_router=w_router, w_exp_gate=w_exp_gate, w_exp_up=w_exp_up, w_exp_down=w_exp_down)
    b, s, d = x.shape
    depth = w_ada.shape[0]
    assert b + 1 <= 8
    cc = jnp.concatenate([c, c_ctx[None, :], jnp.zeros((8 - b - 1, d), F32)], axis=0)
    mod = _ada_mod(cc, w_ada, b_ada)
    rope_tabs = _rope_tables(s)
    hx, hc = x, ctx
    for i in range(depth):
        mod_x = mod[i, :b]
        mod_c = jnp.broadcast_to(mod[i, b:b + 1], (b, mod.shape[2]))
        hx, hc = _layer(i, hx, hc, mod_x, mod_c, p, rope_tabs, i == depth - 1)
    return hx
```
